```python
import math
import jax, jax.numpy as jnp
from jax import lax
import numpy as np

D_MODEL = 1024
BATCH = 4
SEQ = 4096
DEPTH = 1
DEC_BATCH = 128
DEC_SEQ = 4
PAST_LEN = 16384
PAGE_SIZE = 128

N_MEM = 256
EPS = 1e-6
A_HEADS = 8
A_KV_HEADS = 2
A_GROUP = A_HEADS // A_KV_HEADS
A_HD = 64
WINDOW = 128
A_BLOCK = 128
N_BUCKETS = 32
MAX_DISTANCE = 128
B_HEADS = 4
B_DK = 64
B_DV = 128
B_RANK = 16
B_TAU = 16.0
B_CHUNK = 64
C_HEADS = 4
C_HD = 128
P_HEADS = 8
P_NKEYS = 128
P_DKEY = 256
P_TOPK = 16
P_EXPERTS = P_NKEYS * P_NKEYS
P_BLOCK = 256

A_W = A_HEADS * A_HD
A_KVW = A_KV_HEADS * A_HD
B_KW = B_HEADS * B_DK
B_VW = B_HEADS * B_DV
C_W = C_HEADS * C_HD
N_BRANCH = 3
IN_SIZES = (A_W, A_KVW, A_KVW, B_KW, B_KW, B_VW, B_RANK, B_VW, C_W, N_BRANCH * D_MODEL)
IN_OFFSETS = [int(o) for o in np.cumsum(IN_SIZES)[:-1]]
IN_WIDTH = int(sum(IN_SIZES))

kernel_name = 'hybrid_swa_gla_mem_peer_step'


def rmsnorm(x, g):
    xf = x.astype(jnp.float32)
    y = xf * lax.rsqrt(jnp.mean(xf * xf, axis=-1, keepdims=True) + EPS)
    return (y * g.astype(jnp.float32)).astype(x.dtype)


def t5_bucket(dist):
    n = jnp.maximum(dist, 0)
    max_exact = N_BUCKETS // 2
    nf = jnp.maximum(n, 1).astype(jnp.float32)
    large = max_exact + (jnp.log(nf / max_exact) / math.log(MAX_DISTANCE / max_exact)
                         * (N_BUCKETS - max_exact)).astype(jnp.int32)
    large = jnp.minimum(large, N_BUCKETS - 1)
    return jnp.where(n < max_exact, n, large)


def rel_bias_heads(rel_bias, dist):
    q, s = dist.shape
    b = rel_bias.astype(jnp.float32)[t5_bucket(dist)]
    return jnp.transpose(b, (2, 0, 1)).reshape(A_KV_HEADS, A_GROUP, q, s)


def sink_probs(scores, valid, sinks):
    scores = jnp.where(valid, scores, -jnp.inf)
    m = jnp.maximum(jnp.max(scores, axis=-1), sinks)
    p = jnp.exp(scores - m[..., None])
    return p / (jnp.sum(p, axis=-1) + jnp.exp(sinks - m))[..., None]


def window_attn_prompt(q, k, v, sinks, rel_bias):
    B, T = q.shape[:2]
    nb = T // A_BLOCK
    qb = q.reshape(B, nb, A_BLOCK, A_KV_HEADS, A_GROUP, A_HD)
    kc = k.reshape(B, nb, A_BLOCK, A_KV_HEADS, A_HD)
    vc = v.reshape(B, nb, A_BLOCK, A_KV_HEADS, A_HD)
    prev = lambda t: jnp.pad(t[:, :-1], ((0, 0), (1, 0), (0, 0), (0, 0), (0, 0)))
    kb = jnp.concatenate([prev(kc), kc], axis=2)
    vb = jnp.concatenate([prev(vc), vc], axis=2)
    qi = jnp.arange(A_BLOCK)[:, None]
    kj = jnp.arange(2 * A_BLOCK)[None, :]
    dist = qi + A_BLOCK - kj
    band = (dist >= 0) & (dist < WINDOW)
    bias = rel_bias_heads(rel_bias, dist)
    in_cur = kj >= A_BLOCK
    snk = sinks.astype(jnp.float32).reshape(A_KV_HEADS, A_GROUP, 1)
    scale = A_HD ** -0.5

    def one_block(args):
        qblk, kblk, vblk, idx = args
        valid = band & ((idx > 0) | in_cur)
        s = jnp.einsum('bqkgd,bskd->bkgqs', qblk, kblk, preferred_element_type=jnp.float32) * scale + bias
        p = sink_probs(s, valid, snk)
        return jnp.einsum('bkgqs,bskd->bqkgd', p.astype(vblk.dtype), vblk)

    out = lax.map(one_block, (qb.swapaxes(0, 1), kb.swapaxes(0, 1), vb.swapaxes(0, 1), jnp.arange(nb)))
    return out.swapaxes(0, 1).reshape(B, T, A_W)


def window_attn_sample(q, k, v, k_cache, v_cache, sinks, rel_bias):
    B, T = q.shape[:2]
    wc = k_cache.shape[1]
    kk = jnp.concatenate([k_cache.astype(k.dtype), k], axis=1)
    vv = jnp.concatenate([v_cache.astype(v.dtype), v], axis=1)
    dist = (wc + jnp.arange(T))[:, None] - jnp.arange(wc + T)[None, :]
    valid = (dist >= 0) & (dist < WINDOW)
    bias = rel_bias_heads(rel_bias, dist)
    snk = sinks.astype(jnp.float32).reshape(A_KV_HEADS, A_GROUP, 1)
    qg = q.reshape(B, T, A_KV_HEADS, A_GROUP, A_HD)
    s = jnp.einsum('bqkgd,bskd->bkgqs', qg, kk, preferred_element_type=jnp.float32) * A_HD ** -0.5 + bias
    p = sink_probs(s, valid, snk)
    o = jnp.einsum('bkgqs,bskd->bqkgd', p.astype(vv.dtype), vv).reshape(B, T, A_W)
    return o, kk[:, -wc:], vv[:, -wc:]


def gla_chunk(S, inp):
    q, k, v, g = inp
    C = q.shape[1]
    b = jnp.cumsum(g, axis=1)
    o_inter = jnp.einsum('bthk,bhkv->bthv', q * jnp.exp(b), S)
    causal = jnp.tril(jnp.ones((C, C), dtype=bool))[None, :, :, None, None]
    decay = jnp.exp(jnp.where(causal, b[:, :, None] - b[:, None, :], -jnp.inf))
    attn = jnp.einsum('bthk,bshk,btshk->bhts', q, k, decay)
    o_intra = jnp.einsum('bhts,bshv->bthv', attn, v)
    b_last = b[:, -1]
    S_new = jnp.exp(b_last)[..., None] * S + jnp.einsum('bshk,bshv->bhkv', k * jnp.exp(b_last[:, None] - b), v)
    return S_new, o_inter + o_intra


def gla_branch(xq, xk, xv, xlr, xr, S0, w2, gbias, gn_g):
    B, T = xq.shape[:2]
    f32 = jnp.float32
    q = xq.reshape(B, T, B_HEADS, B_DK).astype(f32) * B_DK ** -0.5
    k = xk.reshape(B, T, B_HEADS, B_DK).astype(f32)
    v = xv.reshape(B, T, B_HEADS, B_DV).astype(f32)
    g = (jax.nn.log_sigmoid((xlr @ w2 + gbias).astype(f32)) / B_TAU).reshape(B, T, B_HEADS, B_DK)
    C = math.gcd(T, B_CHUNK)
    nc = T // C
    chunks = lambda t: t.reshape(B, nc, C, *t.shape[2:]).swapaxes(0, 1)
    S, o = lax.scan(gla_chunk, S0.astype(f32), (chunks(q), chunks(k), chunks(v), chunks(g)))
    o = o.swapaxes(0, 1).reshape(B, T, B_HEADS, B_DV)
    o = rmsnorm(o, gn_g).reshape(B, T, B_VW).astype(xr.dtype) * jax.nn.silu(xr)
    return o, S


def memory_kv(mem, norm_g, w_kv, knorm_g):
    B, N, _ = mem.shape
    mk, mv = jnp.split(rmsnorm(mem, norm_g) @ w_kv, 2, axis=-1)
    return rmsnorm(mk.reshape(B, N, C_HEADS, C_HD), knorm_g), mv.reshape(B, N, C_HEADS, C_HD)


def memory_attn(q, mk, mv):
    s = jnp.einsum('bthd,bnhd->bhtn', q, mk.astype(q.dtype), preferred_element_type=jnp.float32) * C_HD ** -0.5
    p = jax.nn.softmax(s, axis=-1)
    return jnp.einsum('bhtn,bnhd->bthd', p.astype(q.dtype), mv.astype(q.dtype))


def peer_ffn(xn, wq, k1, k2, u_tab, v_tab):
    B, T, D = xn.shape
    n = B * T
    nblk = -(-n // P_BLOCK)
    xt = jnp.pad(xn.reshape(n, D), ((0, nblk * P_BLOCK - n), (0, 0))).reshape(nblk, P_BLOCK, D)
    half = P_DKEY // 2
    k1f = k1.astype(jnp.float32)
    k2f = k2.astype(jnp.float32)

    def one_block(xb):
        q = (xb @ wq).astype(jnp.float32).reshape(P_BLOCK, P_HEADS, 2, half)
        s1 = jnp.einsum('thd,nd->thn', q[:, :, 0], k1f)
        s2 = jnp.einsum('thd,nd->thn', q[:, :, 1], k2f)
        v1, i1 = lax.top_k(s1, P_TOPK)
        v2, i2 = lax.top_k(s2, P_TOPK)
        cand = (v1[..., :, None] + v2[..., None, :]).reshape(P_BLOCK, P_HEADS, P_TOPK * P_TOPK)
        cidx = (i1[..., :, None] * P_NKEYS + i2[..., None, :]).reshape(P_BLOCK, P_HEADS, P_TOPK * P_TOPK)
        sc, sel = lax.top_k(cand, P_TOPK)
        eidx = jnp.take_along_axis(cidx, sel, axis=-1)
        gate = jax.nn.softmax(sc, axis=-1)
        hid = jnp.einsum('thkd,td->thk', u_tab[eidx], xb, preferred_element_type=jnp.float32)
        act = (jax.nn.gelu(hid, approximate=False) * gate).astype(xb.dtype)
        return jnp.einsum('thk,thkd->td', act, v_tab[eidx])

    out = lax.map(one_block, xt).reshape(nblk * P_BLOCK, D)[:n]
    return out.reshape(B, T, D)


def decoder_layer(x, p, rel_bias, mem_k, mem_v, win_k, win_v, gla_state):
    B, T, _ = x.shape
    xn = rmsnorm(x, p['norm1_g'])
    aq, ak, av, bq, bk, bv, blr, br, cq, gl = jnp.split(xn @ p['w_in'], IN_OFFSETS, axis=-1)
    aq = rmsnorm(aq.reshape(B, T, A_HEADS, A_HD), p['a_qnorm_g'])
    ak = rmsnorm(ak.reshape(B, T, A_KV_HEADS, A_HD), p['a_knorm_g'])
    av = av.reshape(B, T, A_KV_HEADS, A_HD)
    if win_k is None:
        oa = window_attn_prompt(aq, ak, av, p['a_sinks'], rel_bias)
        new_wk, new_wv = ak[:, -WINDOW:], av[:, -WINDOW:]
        gla_state = jnp.zeros((B, B_HEADS, B_DK, B_DV), jnp.float32)
    else:
        oa, new_wk, new_wv = window_attn_sample(aq, ak, av, win_k, win_v, p['a_sinks'], rel_bias)
    ob, new_S = gla_branch(bq, bk, bv, blr, br, gla_state, p['b_gate_w2'], p['b_gate_b'], p['b_gn_g'])
    cq = rmsnorm(cq.reshape(B, T, C_HEADS, C_HD), p['c_qnorm_g'])
    oc = memory_attn(cq, mem_k, mem_v).reshape(B, T, C_W)
    ga, gb, gc = jnp.split(jax.nn.sigmoid((gl + p['b_gates']).astype(jnp.float32)).astype(x.dtype), N_BRANCH, axis=-1)
    merged = ga * (oa @ p['w_br_a']) + gb * (ob @ p['w_br_b']) + gc * (oc @ p['w_br_c'])
    h = x + merged @ p['w_out']
    y = h + peer_ffn(rmsnorm(h, p['norm2_g']), p['peer_wq'], p['peer_k1'], p['peer_k2'], p['peer_u'], p['peer_v'])
    return y, new_wk, new_wv, new_S.astype(x.dtype)


def setup_inputs(seed: int = 0) -> dict:
    key = jax.random.key(seed)
    keys = jax.random.split(key, 40)
    ctr = [0]

    def nrm(shape, scale):
        k = keys[ctr[0]]
        ctr[0] += 1
        return jax.random.normal(k, shape, jnp.float32) * scale

    def gain(shape):
        return 1.0 + nrm(shape, 0.02)

    wc = min(WINDOW, PAST_LEN)
    L = DEPTH
    D = D_MODEL
    return {
        'x_prompt': nrm((BATCH, SEQ, D), 1.0),
        'x_sample': nrm((DEC_BATCH, DEC_SEQ, D), 1.0),
        'mem_prompt': nrm((BATCH, N_MEM, D), 1.0),
        'cache_win_k': nrm((L, DEC_BATCH, wc, A_KV_HEADS, A_HD), 1.0),
        'cache_win_v': nrm((L, DEC_BATCH, wc, A_KV_HEADS, A_HD), 1.0),
        'state_gla': nrm((L, DEC_BATCH, B_HEADS, B_DK, B_DV), 0.5),
        'cache_mem_k': nrm((L, DEC_BATCH, N_MEM, C_HEADS, C_HD), 1.0),
        'cache_mem_v': nrm((L, DEC_BATCH, N_MEM, C_HEADS, C_HD), 1.0),
        'norm1_g': gain((L, D)),
        'w_in': nrm((L, D, IN_WIDTH), D ** -0.5),
        'b_gates': nrm((L, N_BRANCH * D), 0.01),
        'a_qnorm_g': gain((L, A_HD)),
        'a_knorm_g': gain((L, A_HD)),
        'a_sinks': nrm((L, A_HEADS), 0.5),
        'rel_bias': nrm((N_BUCKETS, A_HEADS), 0.1),
        'b_gate_w2': nrm((L, B_RANK, B_KW), B_RANK ** -0.5),
        'b_gate_b': nrm((L, B_KW), 0.1),
        'b_gn_g': gain((L, B_DV)),
        'c_qnorm_g': gain((L, C_HD)),
        'c_knorm_g': gain((L, C_HD)),
        'mem_norm_g': gain((L, D)),
        'w_mem_kv': nrm((L, D, 2 * C_W), D ** -0.5),
        'w_br_a': nrm((L, A_W, D), A_W ** -0.5),
        'w_br_b': nrm((L, B_VW, D), B_VW ** -0.5),
        'w_br_c': nrm((L, C_W, D), C_W ** -0.5),
        'w_out': nrm((L, D, D), D ** -0.5),
        'norm2_g': gain((L, D)),
        'peer_wq': nrm((L, D, P_HEADS * P_DKEY), D ** -0.5),
        'peer_k1': nrm((L, P_NKEYS, P_DKEY // 2), (P_DKEY // 2) ** -0.5),
        'peer_k2': nrm((L, P_NKEYS, P_DKEY // 2), (P_DKEY // 2) ** -0.5),
        'peer_u': nrm((L, P_EXPERTS, D), D ** -0.5),
        'peer_v': nrm((L, P_EXPERTS, D), 0.5),
    }


def reference(x_prompt, x_sample, mem_prompt, cache_win_k, cache_win_v, state_gla, cache_mem_k, cache_mem_v,
              norm1_g, w_in, b_gates, a_qnorm_g, a_knorm_g, a_sinks, rel_bias, b_gate_w2, b_gate_b, b_gn_g,
              c_qnorm_g, c_knorm_g, mem_norm_g, w_mem_kv, w_br_a, w_br_b, w_br_c, w_out, norm2_g,
              peer_wq, peer_k1, peer_k2, peer_u, peer_v):
    y_p, y_s = x_prompt, x_sample
    wk_p, wv_p, s_p, mk_p, mv_p, wk_s, wv_s, s_s = [], [], [], [], [], [], [], []
    for l in range(DEPTH):
        p = dict(norm1_g=norm1_g[l], w_in=w_in[l], b_gates=b_gates[l], a_qnorm_g=a_qnorm_g[l],
                 a_knorm_g=a_knorm_g[l], a_sinks=a_sinks[l], b_gate_w2=b_gate_w2[l], b_gate_b=b_gate_b[l],
                 b_gn_g=b_gn_g[l], c_qnorm_g=c_qnorm_g[l], w_br_a=w_br_a[l], w_br_b=w_br_b[l],
                 w_br_c=w_br_c[l], w_out=w_out[l], norm2_g=norm2_g[l], peer_wq=peer_wq[l],
                 peer_k1=peer_k1[l], peer_k2=peer_k2[l], peer_u=peer_u[l], peer_v=peer_v[l])
        mk, mv = memory_kv(mem_prompt, mem_norm_g[l], w_mem_kv[l], c_knorm_g[l])
        y_p, nk, nv, ns = decoder_layer(y_p, p, rel_bias, mk, mv, None, None, None)
        wk_p.append(nk); wv_p.append(nv); s_p.append(ns); mk_p.append(mk); mv_p.append(mv)
        y_s, nk, nv, ns = decoder_layer(y_s, p, rel_bias, cache_mem_k[l], cache_mem_v[l],
                                        cache_win_k[l], cache_win_v[l], state_gla[l])
        wk_s.append(nk); wv_s.append(nv); s_s.append(ns)
    new_win_k_prompt = jnp.stack(wk_p)
    new_win_v_prompt = jnp.stack(wv_p)
    new_gla_prompt = jnp.stack(s_p)
    new_mem_k_prompt = jnp.stack(mk_p)
    new_mem_v_prompt = jnp.stack(mv_p)
    new_win_k_sample = jnp.stack(wk_s)
    new_win_v_sample = jnp.stack(wv_s)
    new_gla_sample = jnp.stack(s_s)
    return (y_p, y_s, new_win_k_prompt, new_win_v_prompt, new_gla_prompt, new_mem_k_prompt, new_mem_v_prompt,
            new_win_k_sample, new_win_v_sample, new_gla_sample)
```

```python
import functools
import math

import numpy as np
import jax
import jax.numpy as jnp
from jax import lax
from jax.experimental import pallas as pl
from jax.experimental.pallas import tpu as pltpu

F32 = jnp.float32
BF16 = jnp.bfloat16
I32 = jnp.int32

EPS = 1e-6
D_MODEL = 1024
N_MEM = 256
A_HEADS, A_KV_HEADS, A_GROUP, A_HD = 8, 2, 4, 64
WINDOW = 128
N_BUCKETS, MAX_DISTANCE = 32, 128
B_HEADS, B_DK, B_DV, B_RANK, B_TAU = 4, 64, 128, 16, 16.0
B_CHUNK, B_SUB = 64, 16
C_HEADS, C_HD = 4, 128
P_HEADS, P_NKEYS, P_TOPK = 8, 128, 16
P_EXPERTS = P_NKEYS * P_NKEYS

LANE = 128
COL_GL, COL_AQ, COL_BV, COL_BR, COL_CQ, COL_BQ, COL_BK, COL_AK, COL_AV, COL_LR = (
    0, 3072, 3584, 4096, 4608, 5120, 5376, 5632, 5760, 5888)
IN_COLS = 6144
VMEM_LIMIT = 56 * 1024 * 1024

NT_DIMS = (((1,), (1,)), ((), ()))
TN_DIMS = (((0,), (0,)), ((), ()))


def _rms(x, g):
    ms = jnp.mean(x * x, axis=-1, keepdims=True)
    return x * lax.rsqrt(ms + EPS) * g


def _params(*sem):
    return pltpu.CompilerParams(dimension_semantics=sem, vmem_limit_bytes=VMEM_LIMIT)


def _norm_matmul_kernel(x_ref, g_ref, w_ref, o_ref):
    xn = _rms(x_ref[...], g_ref[...]).astype(BF16)
    o_ref[...] = jnp.dot(xn, w_ref[...], preferred_element_type=F32)


def norm_matmul(x, g, w, tm, tn):
    n, d = x.shape
    m = w.shape[1]
    return pl.pallas_call(
        _norm_matmul_kernel,
        grid=(m // tn, n // tm),
        in_specs=[pl.BlockSpec((tm, d), lambda j, i: (i, 0)),
                  pl.BlockSpec((1, d), lambda j, i: (0, 0)),
                  pl.BlockSpec((d, tn), lambda j, i: (0, j))],
        out_specs=pl.BlockSpec((tm, tn), lambda j, i: (i, j)),
        out_shape=jax.ShapeDtypeStruct((n, m), F32),
        compiler_params=_params("arbitrary", "arbitrary"),
        name="norm_matmul",
    )(x, g.reshape(1, d), w)


def _t5_bucket_np(dist):
    n = np.maximum(dist, 0)
    max_exact = N_BUCKETS // 2
    nf = np.maximum(n, 1).astype(np.float32)
    large = max_exact + (np.log(nf / np.float32(max_exact)) / np.float32(math.log(MAX_DISTANCE / max_exact))
                         * np.float32(N_BUCKETS - max_exact)).astype(np.int32)
    large = np.minimum(large, N_BUCKETS - 1)
    return np.where(n < max_exact, n, large).astype(np.int32)


def _bias_from_buckets(bucket, rb_ref, head):
    tab = jnp.zeros(bucket.shape, F32)
    for bk in range(N_BUCKETS):
        tab = jnp.where(bucket == bk, rb_ref[bk, head], tab)
    return tab


def _swa_prompt_kernel(rb_ref, q_ref, kc_ref, kp_ref, vc_ref, vp_ref, gq_ref, gk_ref, snk_ref, bucket_ref,
                       o_ref, wk_ref, wv_ref, bias_ref):
    blk = WINDOW
    first = jnp.logical_and(pl.program_id(0) == 0, pl.program_id(1) == 0)

    @pl.when(first)
    def _():
        bucket = bucket_ref[...]
        qi = lax.broadcasted_iota(I32, (blk, 2 * blk), 0)
        kj = lax.broadcasted_iota(I32, (blk, 2 * blk), 1)
        dist = qi + blk - kj
        band = jnp.logical_and(dist >= 0, dist < WINDOW)
        for h in range(A_HEADS):
            bias_ref[h] = jnp.where(band, _bias_from_buckets(bucket, rb_ref, h), -jnp.inf)

    i = pl.program_id(1)
    q = q_ref[...]
    kc, kp, vc, vp = kc_ref[...], kp_ref[...], vc_ref[...], vp_ref[...]
    gq, gk = gq_ref[...], gk_ref[...]
    col = lax.broadcasted_iota(I32, (A_GROUP * blk, 2 * blk), 1)
    keep = jnp.logical_or(col >= blk, i > 0)
    scale = A_HD ** -0.5
    outs = [None] * A_HEADS
    kn_out = []
    for g in range(A_KV_HEADS):
        ks = slice(A_HD * g, A_HD * (g + 1))
        kcn = _rms(kc[:, ks], gk)
        kpn = _rms(kp[:, ks], gk)
        kn_out.append(kcn)
        kcat = jnp.concatenate([kpn, kcn], axis=0).astype(BF16)
        vcat = jnp.concatenate([vp[:, ks], vc[:, ks]], axis=0).astype(BF16)
        qg = jnp.concatenate(
            [_rms(q[:, A_HD * (A_GROUP * g + j):A_HD * (A_GROUP * g + j + 1)], gq) for j in range(A_GROUP)], axis=0)
        s = lax.dot_general(qg.astype(BF16), kcat, NT_DIMS, preferred_element_type=F32) * scale
        s = s + bias_ref[A_GROUP * g:A_GROUP * (g + 1)].reshape(A_GROUP * blk, 2 * blk)
        s = jnp.where(keep, s, -jnp.inf)
        snk = snk_ref[g]
        m = jnp.maximum(jnp.max(s, axis=-1, keepdims=True), snk)
        p = jnp.exp(s - m)
        den = jnp.sum(p, axis=-1, keepdims=True) + jnp.exp(snk - m)
        o = jnp.dot((p / den).astype(BF16), vcat, preferred_element_type=F32)
        for j in range(A_GROUP):
            outs[A_GROUP * g + j] = o[blk * j:blk * (j + 1)]
    o_ref[...] = jnp.concatenate(outs, axis=-1)
    wk_ref[0] = jnp.concatenate(kn_out, axis=-1)
    wv_ref[0] = vc


def swa_prompt(proj, batch, seq, rel_bias, gq, gk, sinks):
    blk = WINDOW
    nb = seq // blk
    qi = np.arange(blk)[:, None]
    kj = np.arange(2 * blk)[None, :]
    bucket = jnp.asarray(_t5_bucket_np(qi + blk - kj))
    snk = jnp.broadcast_to(sinks.astype(F32).reshape(A_KV_HEADS, A_GROUP, 1, 1),
                           (A_KV_HEADS, A_GROUP, blk, 1)).reshape(A_KV_HEADS, A_GROUP * blk, 1)
    cur = lambda c: (lambda b, i: (b * nb + i, c))
    prev = lambda c: (lambda b, i: (b * nb + jnp.maximum(i - 1, 0), c))
    kvw = A_KV_HEADS * A_HD
    return pl.pallas_call(
        _swa_prompt_kernel,
        grid=(batch, nb),
        in_specs=[pl.BlockSpec(memory_space=pltpu.SMEM),
                  pl.BlockSpec((blk, A_HEADS * A_HD), cur(COL_AQ // (A_HEADS * A_HD))),
                  pl.BlockSpec((blk, kvw), cur(COL_AK // kvw)),
                  pl.BlockSpec((blk, kvw), prev(COL_AK // kvw)),
                  pl.BlockSpec((blk, kvw), cur(COL_AV // kvw)),
                  pl.BlockSpec((blk, kvw), prev(COL_AV // kvw)),
                  pl.BlockSpec((1, A_HD), lambda b, i: (0, 0)),
                  pl.BlockSpec((1, A_HD), lambda b, i: (0, 0)),
                  pl.BlockSpec((A_KV_HEADS, A_GROUP * blk, 1), lambda b, i: (0, 0, 0)),
                  pl.BlockSpec((blk, 2 * blk), lambda b, i: (0, 0))],
        out_specs=[pl.BlockSpec((blk, A_HEADS * A_HD), lambda b, i: (b * nb + i, 0)),
                   pl.BlockSpec((1, blk, kvw), lambda b, i: (b, 0, 0)),
                   pl.BlockSpec((1, blk, kvw), lambda b, i: (b, 0, 0))],
        out_shape=[jax.ShapeDtypeStruct((batch * seq, A_HEADS * A_HD), F32),
                   jax.ShapeDtypeStruct((batch, blk, kvw), F32),
                   jax.ShapeDtypeStruct((batch, blk, kvw), F32)],
        scratch_shapes=[pltpu.VMEM((A_HEADS, blk, 2 * blk), F32)],
        compiler_params=_params("arbitrary", "arbitrary"),
        name="swa_prompt",
    )(rel_bias.astype(F32), proj, proj, proj, proj, proj, gq.reshape(1, A_HD), gk.reshape(1, A_HD), snk, bucket)


def _swa_sample_kernel(rb_ref, snk_ref, q_ref, kn_ref, vn_ref, ck_ref, cv_ref, gq_ref, gk_ref, bkt_c_ref, bkt_n_ref,
                       o_ref, knew_ref, bias_c_ref, bias_n_ref, *, t_new):
    wc = WINDOW

    @pl.when(pl.program_id(0) == 0)
    def _():
        t = lax.broadcasted_iota(I32, (t_new, LANE), 0)
        c = lax.broadcasted_iota(I32, (t_new, LANE), 1)
        valid_c = c > t
        valid_n = c <= t
        for h in range(A_HEADS):
            bias_c_ref[h] = jnp.where(valid_c, _bias_from_buckets(bkt_c_ref[...], rb_ref, h), -jnp.inf)
            bias_n_ref[h] = jnp.where(valid_n, _bias_from_buckets(bkt_n_ref[...], rb_ref, h), -jnp.inf)

    q = q_ref[...]
    gq, gk = gq_ref[...], gk_ref[...]
    scale = A_HD ** -0.5
    kn_all = []
    kv = []
    for g in range(A_KV_HEADS):
        ks = slice(A_HD * g, A_HD * (g + 1))
        knn = _rms(kn_ref[:, :, ks], gk)
        kn_all.append(knn)
        kv.append((ck_ref[:, :, ks].astype(BF16), cv_ref[:, :, ks].astype(BF16), knn, vn_ref[:, :, ks]))
    outs = []
    for h in range(A_HEADS):
        kc, vc, knn, vnn = kv[h // A_GROUP]
        qh = _rms(q[:, :, A_HD * h:A_HD * (h + 1)], gq)
        s_c = jnp.einsum('bqd,bkd->bqk', qh.astype(BF16), kc, preferred_element_type=F32) * scale + bias_c_ref[h]
        s_n = [jnp.sum(qh * knn[:, j:j + 1, :], axis=-1, keepdims=True) * scale + bias_n_ref[h][:, j:j + 1]
               for j in range(t_new)]
        snk = snk_ref[h]
        m = jnp.maximum(jnp.max(s_c, axis=-1, keepdims=True), snk)
        for j in range(t_new):
            m = jnp.maximum(m, s_n[j])
        p_c = jnp.exp(s_c - m)
        p_n = [jnp.exp(s_n[j] - m) for j in range(t_new)]
        den = jnp.sum(p_c, axis=-1, keepdims=True) + jnp.exp(snk - m)
        for j in range(t_new):
            den = den + p_n[j]
        o = jnp.einsum('bqk,bkd->bqd', (p_c / den).astype(BF16), vc, preferred_element_type=F32)
        for j in range(t_new):
            o = o + (p_n[j] / den) * vnn[:, j:j + 1, :]
        outs.append(o)
    o_ref[...] = jnp.concatenate(outs, axis=-1)
    knew_ref[...] = jnp.concatenate(kn_all, axis=-1)


def swa_sample(proj3, cache_k, cache_v, rel_bias, gq, gk, sinks, bb):
    batch, t_new, _ = proj3.shape
    wc = cache_k.shape[1]
    kvw = A_KV_HEADS * A_HD
    t = np.arange(t_new)[:, None]
    c = np.arange(LANE)[None, :]
    bkt_c = jnp.asarray(_t5_bucket_np(wc + t - c))
    bkt_n = jnp.asarray(_t5_bucket_np(t - c))
    whole = lambda *shape: pl.BlockSpec(shape, lambda i: (0,) * len(shape))
    return pl.pallas_call(
        functools.partial(_swa_sample_kernel, t_new=t_new),
        grid=(batch // bb,),
        in_specs=[pl.BlockSpec(memory_space=pltpu.SMEM),
                  pl.BlockSpec(memory_space=pltpu.SMEM),
                  pl.BlockSpec((bb, t_new, A_HEADS * A_HD), lambda i: (i, 0, COL_AQ // (A_HEADS * A_HD))),
                  pl.BlockSpec((bb, t_new, kvw), lambda i: (i, 0, COL_AK // kvw)),
                  pl.BlockSpec((bb, t_new, kvw), lambda i: (i, 0, COL_AV // kvw)),
                  pl.BlockSpec((bb, wc, kvw), lambda i: (i, 0, 0)),
                  pl.BlockSpec((bb, wc, kvw), lambda i: (i, 0, 0)),
                  whole(1, A_HD), whole(1, A_HD), whole(t_new, LANE), whole(t_new, LANE)],
        out_specs=[pl.BlockSpec((bb, t_new, A_HEADS * A_HD), lambda i: (i, 0, 0)),
                   pl.BlockSpec((bb, t_new, kvw), lambda i: (i, 0, 0))],
        out_shape=[jax.ShapeDtypeStruct((batch, t_new, A_HEADS * A_HD), F32),
                   jax.ShapeDtypeStruct((batch, t_new, kvw), F32)],
        scratch_shapes=[pltpu.VMEM((A_HEADS, t_new, LANE), F32), pltpu.VMEM((A_HEADS, t_new, LANE), F32)],
        compiler_params=_params("arbitrary"),
        name="swa_sample",
    )(rel_bias.astype(F32), sinks.astype(F32), proj3, proj3, proj3, cache_k, cache_v,
      gq.reshape(1, A_HD), gk.reshape(1, A_HD), bkt_c, bkt_n)


def _gla_kernel(q_ref, k_ref, v_ref, lr_ref, r_ref, s0_ref, w2_ref, gb_ref, gn_ref, o_ref, sout_ref, st_ref,
                *, chunk, sub):
    c = pl.program_id(1)

    @pl.when(c == 0)
    def _():
        st_ref[...] = s0_ref[0]

    z = jnp.dot(lr_ref[0].astype(BF16), w2_ref[...], preferred_element_type=F32) + gb_ref[...]
    g = (jnp.minimum(z, 0.0) - jnp.log1p(jnp.exp(-jnp.abs(z)))) / B_TAU
    width = g.shape[-1]
    row_w = lax.broadcasted_iota(I32, (chunk, width), 0)
    b = jnp.zeros_like(g)
    for s in range(chunk):
        b = b + jnp.where(row_w >= s, g[s:s + 1, :], 0.0)
    q = q_ref[0] * (B_DK ** -0.5)
    k, v, r = k_ref[0], v_ref[0], r_ref[0]
    gn = gn_ref[...]
    row = lax.broadcasted_iota(I32, (chunk, chunk), 0)
    col = lax.broadcasted_iota(I32, (chunk, chunk), 1)
    nsub = chunk // sub
    outs = []
    for h in range(B_HEADS):
        ks = slice(B_DK * h, B_DK * (h + 1))
        vs = slice(B_DV * h, B_DV * (h + 1))
        qh, kh, vh, bh = q[:, ks], k[:, ks], v[:, vs], b[:, ks]
        st = st_ref[h]
        o = lax.dot_general((qh * jnp.exp(bh)).astype(BF16), st.astype(BF16), NT_DIMS, preferred_element_type=F32)
        refs = [bh[sub * (j + 1) - 1:sub * (j + 1)] for j in range(nsub)]
        rfull = jnp.concatenate([jnp.broadcast_to(rj, (sub, B_DK)) for rj in refs], axis=0)
        kt = (kh * jnp.exp(rfull - bh)).astype(BF16)
        qcat = jnp.concatenate([qh * jnp.exp(jnp.minimum(bh - rj, 80.0)) for rj in refs], axis=0).astype(BF16)
        aa = lax.dot_general(qcat, kt, NT_DIMS, preferred_element_type=F32)
        att = jnp.zeros((chunk, chunk), F32)
        for j in range(nsub):
            in_sub = jnp.logical_and(col >= sub * j, col < sub * (j + 1))
            att = jnp.where(in_sub, aa[chunk * j:chunk * (j + 1)], att)
        att = jnp.where(col <= row, att, 0.0)
        o = o + jnp.dot(att.astype(BF16), vh.astype(BF16), preferred_element_type=F32)
        bl = bh[chunk - 1:chunk]
        kd = (kh * jnp.exp(bl - bh)).astype(BF16)
        st_ref[h] = st * jnp.exp(bl) + lax.dot_general(vh.astype(BF16), kd, TN_DIMS, preferred_element_type=F32)
        rh = r[:, vs]
        outs.append(_rms(o, gn) * (rh * jax.nn.sigmoid(rh)))
    o_ref[0] = jnp.concatenate(outs, axis=-1)
    sout_ref[0] = st_ref[...]


def gla(proj3, state_t, batch, nchunk, w2p, gbias, gn):
    chunk = proj3.shape[1]
    sub = min(B_SUB, chunk)
    kw, vw = B_HEADS * B_DK, B_HEADS * B_DV
    blk = lambda w, col: pl.BlockSpec((1, chunk, w), lambda b, c: (b * nchunk + c, 0, col // w))
    whole = lambda *shape: pl.BlockSpec(shape, lambda b, c: (0,) * len(shape))
    st_spec = pl.BlockSpec((1, B_HEADS, B_DV, B_DK), lambda b, c: (b, 0, 0, 0))
    return pl.pallas_call(
        functools.partial(_gla_kernel, chunk=chunk, sub=sub),
        grid=(batch, nchunk),
        in_specs=[blk(kw, COL_BQ), blk(kw, COL_BK), blk(vw, COL_BV), blk(LANE, COL_LR), blk(vw, COL_BR), st_spec,
                  whole(LANE, kw), whole(1, kw), whole(1, B_DV)],
        out_specs=[pl.BlockSpec((1, chunk, vw), lambda b, c: (b * nchunk + c, 0, 0)), st_spec],
        out_shape=[jax.ShapeDtypeStruct((batch * nchunk, chunk, vw), F32),
                   jax.ShapeDtypeStruct((batch, B_HEADS, B_DV, B_DK), F32)],
        scratch_shapes=[pltpu.VMEM((B_HEADS, B_DV, B_DK), F32)],
        compiler_params=_params("arbitrary", "arbitrary"),
        name="gla",
    )(proj3, proj3, proj3, proj3, proj3, state_t, w2p, gbias.reshape(1, kw), gn.reshape(1, B_DV))


def _xattn_kernel(q_ref, k_ref, v_ref, gq_ref, gk_ref, o_ref, *rest, bb, norm_k):
    gq, gk = gq_ref[...], gk_ref[...]
    scale = C_HD ** -0.5
    for i in range(bb):
        q, mk, mv = q_ref[i], k_ref[i], v_ref[i]
        outs, kns = [], []
        for h in range(C_HEADS):
            hs = slice(C_HD * h, C_HD * (h + 1))
            kh = mk[:, hs]
            if norm_k:
                kh = _rms(kh, gk)
                kns.append(kh)
            qh = _rms(q[:, hs], gq)
            s = lax.dot_general(qh.astype(BF16), kh.astype(BF16), NT_DIMS, preferred_element_type=F32) * scale
            p = jnp.exp(s - jnp.max(s, axis=-1, keepdims=True))
            p = p / jnp.sum(p, axis=-1, keepdims=True)
            outs.append(jnp.dot(p.astype(BF16), mv[:, hs].astype(BF16), preferred_element_type=F32))
        o_ref[i] = jnp.concatenate(outs, axis=-1)
        if norm_k:
            rest[0][i] = jnp.concatenate(kns, axis=-1)


def xattn(proj3, mem_k, mem_v_arr, v_col, gq, gk, bb, tq, norm_k):
    batch, t, _ = proj3.shape
    cw = C_HEADS * C_HD
    whole = lambda *shape: pl.BlockSpec(shape, lambda b, i: (0,) * len(shape))
    out_specs = [pl.BlockSpec((bb, tq, cw), lambda b, i: (b, i, 0))]
    out_shape = [jax.ShapeDtypeStruct((batch, t, cw), F32)]
    if norm_k:
        out_specs.append(pl.BlockSpec((bb, N_MEM, cw), lambda b, i: (b, 0, 0)))
        out_shape.append(jax.ShapeDtypeStruct((batch, N_MEM, cw), F32))
    return pl.pallas_call(
        functools.partial(_xattn_kernel, bb=bb, norm_k=norm_k),
        grid=(batch // bb, t // tq),
        in_specs=[pl.BlockSpec((bb, tq, cw), lambda b, i: (b, i, COL_CQ // cw)),
                  pl.BlockSpec((bb, N_MEM, cw), lambda b, i: (b, 0, 0)),
                  pl.BlockSpec((bb, N_MEM, cw), lambda b, i: (b, 0, v_col)),
                  whole(1, C_HD), whole(1, C_HD)],
        out_specs=out_specs,
        out_shape=out_shape,
        compiler_params=_params("arbitrary", "arbitrary"),
        name="xattn",
    )(proj3, mem_k, mem_v_arr, gq.reshape(1, C_HD), gk.reshape(1, C_HD))


def _merge_kernel(oa_ref, ob_ref, oc_ref, gl_ref, bg_ref, x_ref, wa_ref, wb_ref, wc_ref, wo_ref, g2_ref,
                  h_ref, xn_ref):
    d = D_MODEL
    gate = jax.nn.sigmoid(gl_ref[...] + bg_ref[...])
    pa = jnp.dot(oa_ref[...].astype(BF16), wa_ref[...], preferred_element_type=F32)
    pb = jnp.dot(ob_ref[...].astype(BF16), wb_ref[...], preferred_element_type=F32)
    pc = jnp.dot(oc_ref[...].astype(BF16), wc_ref[...], preferred_element_type=F32)
    merged = gate[:, :d] * pa + gate[:, d:2 * d] * pb + gate[:, 2 * d:] * pc
    h = x_ref[...] + jnp.dot(merged.astype(BF16), wo_ref[...], preferred_element_type=F32)
    h_ref[...] = h
    xn_ref[...] = _rms(h, g2_ref[...]).astype(BF16)


def merge(oa, ob, oc, proj, b_gates, x, wa, wb, wc, wo, g2, tm):
    n, d = x.shape
    bw = oa.shape[1]
    tok = lambda w: pl.BlockSpec((tm, w), lambda i: (i, 0))
    whole = lambda *shape: pl.BlockSpec(shape, lambda i: (0,) * len(shape))
    return pl.pallas_call(
        _merge_kernel,
        grid=(n // tm,),
        in_specs=[tok(bw), tok(bw), tok(bw), tok(3 * d), whole(1, 3 * d), tok(d),
                  whole(bw, d), whole(bw, d), whole(bw, d), whole(d, d), whole(1, d)],
        out_specs=[tok(d), tok(d)],
        out_shape=[jax.ShapeDtypeStruct((n, d), F32), jax.ShapeDtypeStruct((n, d), BF16)],
        compiler_params=_params("arbitrary"),
        name="merge",
    )(oa, ob, oc, proj, b_gates.reshape(1, 3 * d), x, wa, wb, wc, wo, g2.reshape(1, d))


def _top16(s):
    nrow = s.shape[0]
    row = lax.broadcasted_iota(I32, s.shape, 0)
    rank = jnp.full(s.shape, P_NKEYS, I32)
    vals = []
    for kk in range(P_TOPK):
        m = jnp.max(s, axis=0, keepdims=True)
        idx = jnp.min(jnp.where(s == m, row, nrow), axis=0, keepdims=True)
        hit = row == idx
        rank = jnp.where(hit, kk, rank)
        s = jnp.where(hit, -jnp.inf, s)
        vals.append(m)
    return jnp.concatenate(vals, axis=0), rank


_CAND_ROWS = 16 + 7 * 8 + 8


def _peer_route_kernel(xn_ref, wq_ref, k1_ref, k2_ref, nb_ref, e1_ref, r2_ref, e2_ref, s_ref):
    tb = xn_ref.shape[0]
    qt = lax.dot_general(wq_ref[...], xn_ref[...], NT_DIMS, preferred_element_type=F32)
    half = qt.shape[0] // 2
    s_ref[0] = jnp.dot(k1_ref[...], qt[:half].astype(BF16), preferred_element_type=F32)
    s_ref[1] = jnp.dot(k2_ref[...], qt[half:].astype(BF16), preferred_element_type=F32)
    for st in range(tb // LANE):
        ls = slice(LANE * st, LANE * (st + 1))
        s1 = s_ref[0, :, ls]
        s2 = s_ref[1, :, ls]
        v1, rank1 = _top16(s1)
        v2, rank2 = _top16(s2)
        blocks = [v1[0:1] + v2[0:16]]
        blocks += [v1[a:a + 1] + v2[0:8] for a in range(1, 8)]
        blocks.append(v1[8:16] + v2[0:1])
        cand = jnp.concatenate(blocks, axis=0)
        _, crank = _top16(cand)
        sel = crank < P_TOPK
        cmax = v1[0:1] + v2[0:1]
        zsum = jnp.sum(jnp.where(sel, jnp.exp(cand - cmax), 0.0), axis=0, keepdims=True)
        self32 = sel.astype(F32)
        nb = jnp.zeros(s1.shape, F32)
        nb = jnp.where(rank1 == 0, jnp.sum(self32[0:16], axis=0, keepdims=True), nb)
        for a in range(1, 8):
            cnt = jnp.sum(self32[16 + 8 * (a - 1):16 + 8 * a], axis=0, keepdims=True)
            nb = jnp.where(rank1 == a, cnt, nb)
        for a in range(8, 16):
            nb = jnp.where(rank1 == a, self32[72 + a - 8:73 + a - 8], nb)
        nb_ref[0, :, ls] = nb
        e1_ref[0, :, ls] = jnp.where(rank1 < P_TOPK, jnp.exp(s1 - v1[0:1]), 0.0)
        r2_ref[0, :, ls] = rank2.astype(F32)
        e2_ref[0, :, ls] = jnp.where(rank2 < P_TOPK, jnp.exp(s2 - v2[0:1]) / zsum, 0.0)


def peer_route(xn, wq_t, k1, k2, tb):
    n, d = xn.shape
    hw = wq_t.shape[0] // P_HEADS
    out = jax.ShapeDtypeStruct((P_HEADS, P_NKEYS, n), F32)
    ospec = pl.BlockSpec((1, P_NKEYS, tb), lambda i, h: (h, 0, i))
    return pl.pallas_call(
        _peer_route_kernel,
        grid=(n // tb, P_HEADS),
        in_specs=[pl.BlockSpec((tb, d), lambda i, h: (i, 0)),
                  pl.BlockSpec((hw, d), lambda i, h: (h, 0)),
                  pl.BlockSpec((P_NKEYS, hw // 2), lambda i, h: (0, 0)),
                  pl.BlockSpec((P_NKEYS, hw // 2), lambda i, h: (0, 0))],
        out_specs=[ospec] * 4,
        out_shape=[out] * 4,
        scratch_shapes=[pltpu.VMEM((2, P_NKEYS, tb), F32)],
        compiler_params=_params("arbitrary", "arbitrary"),
        name="peer_route",
    )(xn, wq_t, k1, k2)


def _peer_dense_kernel(xn_ref, u_ref, vt_ref, nb_ref, e1_ref, r2_ref, e2_ref, h_ref, y_ref, acc_ref, act_ref, hid_ref):
    j = pl.program_id(1)
    tb = xn_ref.shape[0]
    eb = u_ref.shape[0]

    @pl.when(j == 0)
    def _():
        acc_ref[...] = jnp.zeros_like(acc_ref)

    hid_ref[...] = lax.dot_general(u_ref[...], xn_ref[...], NT_DIMS, preferred_element_type=F32)
    n1 = eb // P_NKEYS
    assert n1 == 8
    grp = pl.ds(pl.multiple_of(j * n1, n1), n1)
    for c in range(n1):
        rs = slice(P_NKEYS * c, P_NKEYS * (c + 1))
        for st in range(tb // LANE):
            ls = slice(LANE * st, LANE * (st + 1))
            gsum = jnp.zeros((P_NKEYS, LANE), F32)
            for h in range(P_HEADS):
                nbrow = nb_ref[h, grp, ls][c:c + 1]
                e1row = e1_ref[h, grp, ls][c:c + 1]
                gsum = gsum + jnp.where(r2_ref[h, :, ls] < nbrow, e2_ref[h, :, ls], 0.0) * e1row
            hh = hid_ref[rs, ls]
            act = 0.5 * hh * (1.0 + lax.erf(hh * math.sqrt(0.5))) * gsum
            act_ref[rs, ls] = act.astype(BF16)
    acc_ref[...] += jnp.dot(vt_ref[...], act_ref[...], preferred_element_type=F32)

    @pl.when(j == pl.num_programs(1) - 1)
    def _():
        y_ref[...] = h_ref[...] + acc_ref[...].T


def peer_dense(xn, u, vt, nb, e1, r2, e2, h, tb, eb):
    n, d = xn.shape
    ne = u.shape[0]
    rspec = pl.BlockSpec((P_HEADS, P_NKEYS, tb), lambda i, j: (0, 0, i))
    return pl.pallas_call(
        _peer_dense_kernel,
        grid=(n // tb, ne // eb),
        in_specs=[pl.BlockSpec((tb, d), lambda i, j: (i, 0)),
                  pl.BlockSpec((eb, d), lambda i, j: (j, 0)),
                  pl.BlockSpec((d, eb), lambda i, j: (0, j)),
                  rspec, rspec, rspec, rspec,
                  pl.BlockSpec((tb, d), lambda i, j: (i, 0))],
        out_specs=pl.BlockSpec((tb, d), lambda i, j: (i, 0)),
        out_shape=jax.ShapeDtypeStruct((n, d), F32),
        scratch_shapes=[pltpu.VMEM((d, tb), F32), pltpu.VMEM((eb, tb), BF16), pltpu.VMEM((eb, tb), F32)],
        compiler_params=_params("arbitrary", "arbitrary"),
        name="peer_dense",
    )(xn, u, vt, nb, e1, r2, e2, h)


TOKEN_TILE = 512
PEER_EXPERT_TILE = 8 * P_NKEYS
SAMPLE_BATCH_TILE = 8


def _permute_w_in(w_in):
    d = w_in.shape[0]
    seg = lambda a, b: w_in[:, a:b]
    lr = jnp.pad(seg(1792, 1808), ((0, 0), (0, LANE - B_RANK)))
    cols = [seg(2832, 5904), seg(0, 512), seg(1280, 1792), seg(1808, 2320), seg(2320, 2832), seg(768, 1024),
            seg(1024, 1280), seg(512, 640), seg(640, 768), lr, jnp.zeros((d, IN_COLS - COL_LR - LANE), w_in.dtype)]
    return jnp.concatenate(cols, axis=1).astype(BF16)


def kernel(x_prompt, x_sample, mem_prompt, cache_win_k, cache_win_v, state_gla, cache_mem_k, cache_mem_v, norm1_g, w_in, b_gates, a_qnorm_g, a_knorm_g, a_sinks, rel_bias, b_gate_w2, b_gate_b, b_gn_g, c_qnorm_g, c_knorm_g, mem_norm_g, w_mem_kv, w_br_a, w_br_b, w_br_c, w_out, norm2_g, peer_wq, peer_k1, peer_k2, peer_u, peer_v):
    depth = w_in.shape[0]
    assert depth == 1, "one layer"
    l = 0
    bp, tp, d = x_prompt.shape
    bs, ts, _ = x_sample.shape
    cw = C_HEADS * C_HD
    kvw = A_KV_HEADS * A_HD

    w_all = _permute_w_in(w_in[l])
    w2p = jnp.pad(b_gate_w2[l], ((0, LANE - B_RANK), (0, 0))).astype(BF16)
    wa, wb, wc, wo = (w.astype(BF16) for w in (w_br_a[l], w_br_b[l], w_br_c[l], w_out[l]))
    wq_t = peer_wq[l].T.astype(BF16)
    k1, k2 = peer_k1[l].astype(BF16), peer_k2[l].astype(BF16)
    u = peer_u[l].astype(BF16)
    vt = peer_v[l].T.astype(BF16)

    def channel_mix(oa, ob, oc, proj, x2):
        tm = min(TOKEN_TILE, x2.shape[0])
        h, xn2 = merge(oa, ob, oc, proj, b_gates[l], x2, wa, wb, wc, wo, norm2_g[l], tm)
        nb, e1, r2, e2 = peer_route(xn2, wq_t, k1, k2, tm)
        return peer_dense(xn2, u, vt, nb, e1, r2, e2, h, tm, PEER_EXPERT_TILE)

    xp = x_prompt.reshape(bp * tp, d)
    proj_p = norm_matmul(xp, norm1_g[l], w_all, min(TOKEN_TILE, bp * tp), IN_COLS // 2)
    memkv = norm_matmul(mem_prompt.reshape(bp * N_MEM, d), mem_norm_g[l], w_mem_kv[l].astype(BF16), min(TOKEN_TILE, bp * N_MEM), 2 * cw)
    oa_p, wk_p, wv_p = swa_prompt(proj_p, bp, tp, rel_bias, a_qnorm_g[l], a_knorm_g[l], a_sinks[l])
    nchunk = tp // B_CHUNK
    ob_p, st_p = gla(proj_p.reshape(bp * nchunk, B_CHUNK, IN_COLS), jnp.zeros((bp, B_HEADS, B_DV, B_DK), F32),
                     bp, nchunk, w2p, b_gate_b[l], b_gn_g[l])
    memkv3 = memkv.reshape(bp, N_MEM, 2 * cw)
    oc_p, mk_p = xattn(proj_p.reshape(bp, tp, IN_COLS), memkv3, memkv3, 1, c_qnorm_g[l], c_knorm_g[l],
                       1, min(TOKEN_TILE, tp), True)
    y_p = channel_mix(oa_p, ob_p.reshape(bp * tp, -1), oc_p.reshape(bp * tp, cw), proj_p, xp)

    xs = x_sample.reshape(bs * ts, d)
    proj_s = norm_matmul(xs, norm1_g[l], w_all, min(TOKEN_TILE, bs * ts), IN_COLS // 2)
    proj_s3 = proj_s.reshape(bs, ts, IN_COLS)
    ck = cache_win_k[l].reshape(bs, -1, kvw)
    cv = cache_win_v[l].reshape(bs, -1, kvw)
    oa_s, knew = swa_sample(proj_s3, ck, cv, rel_bias, a_qnorm_g[l], a_knorm_g[l], a_sinks[l], SAMPLE_BATCH_TILE)
    vnew = proj_s3[:, :, COL_AV:COL_AV + kvw]
    ob_s, st_s = gla(proj_s3, jnp.swapaxes(state_gla[l], -1, -2), bs, 1, w2p, b_gate_b[l], b_gn_g[l])
    oc_s, = xattn(proj_s3, cache_mem_k[l].reshape(bs, N_MEM, cw), cache_mem_v[l].reshape(bs, N_MEM, cw), 0,
                  c_qnorm_g[l], c_knorm_g[l], SAMPLE_BATCH_TILE, ts, False)
    y_s = channel_mix(oa_s.reshape(bs * ts, -1), ob_s.reshape(bs * ts, -1), oc_s.reshape(bs * ts, cw), proj_s, xs)

    wc_len = ck.shape[1]
    new_wk_s = jnp.concatenate([ck, knew], axis=1)[:, -wc_len:]
    new_wv_s = jnp.concatenate([cv, vnew], axis=1)[:, -wc_len:]
    return (y_p.reshape(bp, tp, d), y_s.reshape(bs, ts, d),
            wk_p.reshape(1, bp, WINDOW, A_KV_HEADS, A_HD), wv_p.reshape(1, bp, WINDOW, A_KV_HEADS, A_HD),
            jnp.swapaxes(st_p, -1, -2)[None],
            mk_p.reshape(1, bp, N_MEM, C_HEADS, C_HD), memkv3[:, :, cw:].reshape(1, bp, N_MEM, C_HEADS, C_HD),
            new_wk_s.reshape(1, bs, wc_len, A_KV_HEADS, A_HD), new_wv_s.reshape(1, bs, wc_len, A_KV_HEADS, A_HD),
            jnp.swapaxes(st_s, -1, -2)[None])
```

```python
import functools
import math

import numpy as np
import jax
import jax.numpy as jnp
from jax import lax
from jax.experimental import pallas as pl
from jax.experimental.pallas import tpu as pltpu

F32 = jnp.float32
BF16 = jnp.bfloat16
I32 = jnp.int32

EPS = 1e-6
D_MODEL = 1024
N_MEM = 256
A_HEADS, A_KV_HEADS, A_GROUP, A_HD = 8, 2, 4, 64
WINDOW = 128
N_BUCKETS, MAX_DISTANCE = 32, 128
B_HEADS, B_DK, B_DV, B_RANK, B_TAU = 4, 64, 128, 16, 16.0
B_CHUNK, B_SUB = 64, 16
C_HEADS, C_HD = 4, 128
P_HEADS, P_NKEYS, P_TOPK = 8, 128, 16
P_EXPERTS = P_NKEYS * P_NKEYS

LANE = 128
COL_GL, COL_AQ, COL_BV, COL_BR, COL_CQ, COL_BQ, COL_BK, COL_AK, COL_AV, COL_LR = (
    0, 3072, 3584, 4096, 4608, 5120, 5376, 5632, 5760, 5888)
IN_COLS = 6144
VMEM_LIMIT = 56 * 1024 * 1024

NT_DIMS = (((1,), (1,)), ((), ()))
TN_DIMS = (((0,), (0,)), ((), ()))


def _rms(x, g):
    ms = jnp.mean(x * x, axis=-1, keepdims=True)
    return x * lax.rsqrt(ms + EPS) * g


def _params(*sem):
    return pltpu.CompilerParams(dimension_semantics=sem, vmem_limit_bytes=VMEM_LIMIT)


def _norm_matmul_kernel(x_ref, g_ref, w_ref, o_ref):
    xn = _rms(x_ref[...], g_ref[...]).astype(BF16)
    o_ref[...] = jnp.dot(xn, w_ref[...], preferred_element_type=F32)


def norm_matmul(x, g, w, tm, tn):
    n, d = x.shape
    m = w.shape[1]
    return pl.pallas_call(
        _norm_matmul_kernel,
        grid=(m // tn, n // tm),
        in_specs=[pl.BlockSpec((tm, d), lambda j, i: (i, 0)),
                  pl.BlockSpec((1, d), lambda j, i: (0, 0)),
                  pl.BlockSpec((d, tn), lambda j, i: (0, j))],
        out_specs=pl.BlockSpec((tm, tn), lambda j, i: (i, j)),
        out_shape=jax.ShapeDtypeStruct((n, m), F32),
        compiler_params=_params("arbitrary", "arbitrary"),
        name="norm_matmul",
    )(x, g.reshape(1, d), w)


def _t5_bucket_np(dist):
    n = np.maximum(dist, 0)
    max_exact = N_BUCKETS // 2
    nf = np.maximum(n, 1).astype(np.float32)
    large = max_exact + (np.log(nf / np.float32(max_exact)) / np.float32(math.log(MAX_DISTANCE / max_exact))
                         * np.float32(N_BUCKETS - max_exact)).astype(np.int32)
    large = np.minimum(large, N_BUCKETS - 1)
    return np.where(n < max_exact, n, large).astype(np.int32)


def _bias_from_buckets(bucket, rb_ref, head):
    tab = jnp.zeros(bucket.shape, F32)
    for bk in range(N_BUCKETS):
        tab = jnp.where(bucket == bk, rb_ref[bk, head], tab)
    return tab


def _swa_prompt_kernel(rb_ref, q_ref, kc_ref, kp_ref, vc_ref, vp_ref, gq_ref, gk_ref, snk_ref, bucket_ref,
                       o_ref, wk_ref, wv_ref, bias_ref):
    blk = WINDOW
    first = jnp.logical_and(pl.program_id(0) == 0, pl.program_id(1) == 0)

    @pl.when(first)
    def _():
        bucket = bucket_ref[...]
        qi = lax.broadcasted_iota(I32, (blk, 2 * blk), 0)
        kj = lax.broadcasted_iota(I32, (blk, 2 * blk), 1)
        dist = qi + blk - kj
        band = jnp.logical_and(dist >= 0, dist < WINDOW)
        for h in range(A_HEADS):
            bias_ref[h] = jnp.where(band, _bias_from_buckets(bucket, rb_ref, h), -jnp.inf)

    i = pl.program_id(1)
    q = q_ref[...]
    kc, kp, vc, vp = kc_ref[...], kp_ref[...], vc_ref[...], vp_ref[...]
    gq, gk = gq_ref[...], gk_ref[...]
    col = lax.broadcasted_iota(I32, (A_GROUP * blk, 2 * blk), 1)
    keep = jnp.logical_or(col >= blk, i > 0)
    scale = A_HD ** -0.5
    outs = [None] * A_HEADS
    kn_out = []
    for g in range(A_KV_HEADS):
        ks = slice(A_HD * g, A_HD * (g + 1))
        kcn = _rms(kc[:, ks], gk)
        kpn = _rms(kp[:, ks], gk)
        kn_out.append(kcn)
        kcat = jnp.concatenate([kpn, kcn], axis=0).astype(BF16)
        vcat = jnp.concatenate([vp[:, ks], vc[:, ks]], axis=0).astype(BF16)
        qg = jnp.concatenate(
            [_rms(q[:, A_HD * (A_GROUP * g + j):A_HD * (A_GROUP * g + j + 1)], gq) for j in range(A_GROUP)], axis=0)
        s = lax.dot_general(qg.astype(BF16), kcat, NT_DIMS, preferred_element_type=F32) * scale
        s = s + bias_ref[A_GROUP * g:A_GROUP * (g + 1)].reshape(A_GROUP * blk, 2 * blk)
        s = jnp.where(keep, s, -jnp.inf)
        snk = snk_ref[g]
        m = jnp.maximum(jnp.max(s, axis=-1, keepdims=True), snk)
        p = jnp.exp(s - m)
        den = jnp.sum(p, axis=-1, keepdims=True) + jnp.exp(snk - m)
        o = jnp.dot((p / den).astype(BF16), vcat, preferred_element_type=F32)
        for j in range(A_GROUP):
            outs[A_GROUP * g + j] = o[blk * j:blk * (j + 1)]
    o_ref[...] = jnp.concatenate(outs, axis=-1)
    wk_ref[0] = jnp.concatenate(kn_out, axis=-1)
    wv_ref[0] = vc


def swa_prompt(proj, batch, seq, rel_bias, gq, gk, sinks):
    blk = WINDOW
    nb = seq // blk
    qi = np.arange(blk)[:, None]
    kj = np.arange(2 * blk)[None, :]
    bucket = jnp.asarray(_t5_bucket_np(qi + blk - kj))
    snk = jnp.broadcast_to(sinks.astype(F32).reshape(A_KV_HEADS, A_GROUP, 1, 1),
                           (A_KV_HEADS, A_GROUP, blk, 1)).reshape(A_KV_HEADS, A_GROUP * blk, 1)
    cur = lambda c: (lambda b, i: (b * nb + i, c))
    prev = lambda c: (lambda b, i: (b * nb + jnp.maximum(i - 1, 0), c))
    kvw = A_KV_HEADS * A_HD
    return pl.pallas_call(
        _swa_prompt_kernel,
        grid=(batch, nb),
        in_specs=[pl.BlockSpec(memory_space=pltpu.SMEM),
                  pl.BlockSpec((blk, A_HEADS * A_HD), cur(COL_AQ // (A_HEADS * A_HD))),
                  pl.BlockSpec((blk, kvw), cur(COL_AK // kvw)),
                  pl.BlockSpec((blk, kvw), prev(COL_AK // kvw)),
                  pl.BlockSpec((blk, kvw), cur(COL_AV // kvw)),
                  pl.BlockSpec((blk, kvw), prev(COL_AV // kvw)),
                  pl.BlockSpec((1, A_HD), lambda b, i: (0, 0)),
                  pl.BlockSpec((1, A_HD), lambda b, i: (0, 0)),
                  pl.BlockSpec((A_KV_HEADS, A_GROUP * blk, 1), lambda b, i: (0, 0, 0)),
                  pl.BlockSpec((blk, 2 * blk), lambda b, i: (0, 0))],
        out_specs=[pl.BlockSpec((blk, A_HEADS * A_HD), lambda b, i: (b * nb + i, 0)),
                   pl.BlockSpec((1, blk, kvw), lambda b, i: (b, 0, 0)),
                   pl.BlockSpec((1, blk, kvw), lambda b, i: (b, 0, 0))],
        out_shape=[jax.ShapeDtypeStruct((batch * seq, A_HEADS * A_HD), F32),
                   jax.ShapeDtypeStruct((batch, blk, kvw), F32),
                   jax.ShapeDtypeStruct((batch, blk, kvw), F32)],
        scratch_shapes=[pltpu.VMEM((A_HEADS, blk, 2 * blk), F32)],
        compiler_params=_params("arbitrary", "arbitrary"),
        name="swa_prompt",
    )(rel_bias.astype(F32), proj, proj, proj, proj, proj, gq.reshape(1, A_HD), gk.reshape(1, A_HD), snk, bucket)


def _swa_sample_kernel(rb_ref, snk_ref, q_ref, kn_ref, vn_ref, ck_ref, cv_ref, gq_ref, gk_ref, bkt_c_ref, bkt_n_ref,
                       o_ref, knew_ref, bias_c_ref, bias_n_ref, *, t_new):
    wc = WINDOW

    @pl.when(pl.program_id(0) == 0)
    def _():
        t = lax.broadcasted_iota(I32, (t_new, LANE), 0)
        c = lax.broadcasted_iota(I32, (t_new, LANE), 1)
        valid_c = c > t
        valid_n = c <= t
        for h in range(A_HEADS):
            bias_c_ref[h] = jnp.where(valid_c, _bias_from_buckets(bkt_c_ref[...], rb_ref, h), -jnp.inf)
            bias_n_ref[h] = jnp.where(valid_n, _bias_from_buckets(bkt_n_ref[...], rb_ref, h), -jnp.inf)

    q = q_ref[...]
    gq, gk = gq_ref[...], gk_ref[...]
    scale = A_HD ** -0.5
    kn_all = []
    kv = []
    for g in range(A_KV_HEADS):
        ks = slice(A_HD * g, A_HD * (g + 1))
        knn = _rms(kn_ref[:, :, ks], gk)
        kn_all.append(knn)
        kv.append((ck_ref[:, :, ks].astype(BF16), cv_ref[:, :, ks].astype(BF16), knn, vn_ref[:, :, ks]))
    outs = []
    for h in range(A_HEADS):
        kc, vc, knn, vnn = kv[h // A_GROUP]
        qh = _rms(q[:, :, A_HD * h:A_HD * (h + 1)], gq)
        s_c = jnp.einsum('bqd,bkd->bqk', qh.astype(BF16), kc, preferred_element_type=F32) * scale + bias_c_ref[h]
        s_n = [jnp.sum(qh * knn[:, j:j + 1, :], axis=-1, keepdims=True) * scale + bias_n_ref[h][:, j:j + 1]
               for j in range(t_new)]
        snk = snk_ref[h]
        m = jnp.maximum(jnp.max(s_c, axis=-1, keepdims=True), snk)
        for j in range(t_new):
            m = jnp.maximum(m, s_n[j])
        p_c = jnp.exp(s_c - m)
        p_n = [jnp.exp(s_n[j] - m) for j in range(t_new)]
        den = jnp.sum(p_c, axis=-1, keepdims=True) + jnp.exp(snk - m)
        for j in range(t_new):
            den = den + p_n[j]
        o = jnp.einsum('bqk,bkd->bqd', (p_c / den).astype(BF16), vc, preferred_element_type=F32)
        for j in range(t_new):
            o = o + (p_n[j] / den) * vnn[:, j:j + 1, :]
        outs.append(o)
    o_ref[...] = jnp.concatenate(outs, axis=-1)
    knew_ref[...] = jnp.concatenate(kn_all, axis=-1)


def swa_sample(proj3, cache_k, cache_v, rel_bias, gq, gk, sinks, bb):
    batch, t_new, _ = proj3.shape
    wc = cache_k.shape[1]
    kvw = A_KV_HEADS * A_HD
    t = np.arange(t_new)[:, None]
    c = np.arange(LANE)[None, :]
    bkt_c = jnp.asarray(_t5_bucket_np(wc + t - c))
    bkt_n = jnp.asarray(_t5_bucket_np(t - c))
    whole = lambda *shape: pl.BlockSpec(shape, lambda i: (0,) * len(shape))
    return pl.pallas_call(
        functools.partial(_swa_sample_kernel, t_new=t_new),
        grid=(batch // bb,),
        in_specs=[pl.BlockSpec(memory_space=pltpu.SMEM),
                  pl.BlockSpec(memory_space=pltpu.SMEM),
                  pl.BlockSpec((bb, t_new, A_HEADS * A_HD), lambda i: (i, 0, COL_AQ // (A_HEADS * A_HD))),
                  pl.BlockSpec((bb, t_new, kvw), lambda i: (i, 0, COL_AK // kvw)),
                  pl.BlockSpec((bb, t_new, kvw), lambda i: (i, 0, COL_AV // kvw)),
                  pl.BlockSpec((bb, wc, kvw), lambda i: (i, 0, 0)),
                  pl.BlockSpec((bb, wc, kvw), lambda i: (i, 0, 0)),
                  whole(1, A_HD), whole(1, A_HD), whole(t_new, LANE), whole(t_new, LANE)],
        out_specs=[pl.BlockSpec((bb, t_new, A_HEADS * A_HD), lambda i: (i, 0, 0)),
                   pl.BlockSpec((bb, t_new, kvw), lambda i: (i, 0, 0))],
        out_shape=[jax.ShapeDtypeStruct((batch, t_new, A_HEADS * A_HD), F32),
                   jax.ShapeDtypeStruct((batch, t_new, kvw), F32)],
        scratch_shapes=[pltpu.VMEM((A_HEADS, t_new, LANE), F32), pltpu.VMEM((A_HEADS, t_new, LANE), F32)],
        compiler_params=_params("arbitrary"),
        name="swa_sample",
    )(rel_bias.astype(F32), sinks.astype(F32), proj3, proj3, proj3, cache_k, cache_v,
      gq.reshape(1, A_HD), gk.reshape(1, A_HD), bkt_c, bkt_n)


def _gla_kernel(q_ref, k_ref, v_ref, lr_ref, r_ref, s0_ref, w2_ref, gb_ref, gn_ref, o_ref, sout_ref, st_ref,
                *, chunk, sub):
    c = pl.program_id(1)

    @pl.when(c == 0)
    def _():
        st_ref[...] = s0_ref[0]

    z = jnp.dot(lr_ref[0].astype(BF16), w2_ref[...], preferred_element_type=F32) + gb_ref[...]
    g = (jnp.minimum(z, 0.0) - jnp.log1p(jnp.exp(-jnp.abs(z)))) / B_TAU
    width = g.shape[-1]
    row_w = lax.broadcasted_iota(I32, (chunk, width), 0)
    b = jnp.zeros_like(g)
    for s in range(chunk):
        b = b + jnp.where(row_w >= s, g[s:s + 1, :], 0.0)
    q = q_ref[0] * (B_DK ** -0.5)
    k, v, r = k_ref[0], v_ref[0], r_ref[0]
    gn = gn_ref[...]
    row = lax.broadcasted_iota(I32, (chunk, chunk), 0)
    col = lax.broadcasted_iota(I32, (chunk, chunk), 1)
    nsub = chunk // sub
    outs = []
    for h in range(B_HEADS):
        ks = slice(B_DK * h, B_DK * (h + 1))
        vs = slice(B_DV * h, B_DV * (h + 1))
        qh, kh, vh, bh = q[:, ks], k[:, ks], v[:, vs], b[:, ks]
        st = st_ref[h]
        o = lax.dot_general((qh * jnp.exp(bh)).astype(BF16), st.astype(BF16), NT_DIMS, preferred_element_type=F32)
        refs = [bh[sub * (j + 1) - 1:sub * (j + 1)] for j in range(nsub)]
        rfull = jnp.concatenate([jnp.broadcast_to(rj, (sub, B_DK)) for rj in refs], axis=0)
        kt = (kh * jnp.exp(rfull - bh)).astype(BF16)
        qcat = jnp.concatenate([qh * jnp.exp(jnp.minimum(bh - rj, 80.0)) for rj in refs], axis=0).astype(BF16)
        aa = lax.dot_general(qcat, kt, NT_DIMS, preferred_element_type=F32)
        att = jnp.zeros((chunk, chunk), F32)
        for j in range(nsub):
            in_sub = jnp.logical_and(col >= sub * j, col < sub * (j + 1))
            att = jnp.where(in_sub, aa[chunk * j:chunk * (j + 1)], att)
        att = jnp.where(col <= row, att, 0.0)
        o = o + jnp.dot(att.astype(BF16), vh.astype(BF16), preferred_element_type=F32)
        bl = bh[chunk - 1:chunk]
        kd = (kh * jnp.exp(bl - bh)).astype(BF16)
        st_ref[h] = st * jnp.exp(bl) + lax.dot_general(vh.astype(BF16), kd, TN_DIMS, preferred_element_type=F32)
        rh = r[:, vs]
        outs.append(_rms(o, gn) * (rh * jax.nn.sigmoid(rh)))
    o_ref[0] = jnp.concatenate(outs, axis=-1)
    sout_ref[0] = st_ref[...]


def gla(proj3, state_t, batch, nchunk, w2p, gbias, gn):
    chunk = proj3.shape[1]
    sub = min(B_SUB, chunk)
    kw, vw = B_HEADS * B_DK, B_HEADS * B_DV
    blk = lambda w, col: pl.BlockSpec((1, chunk, w), lambda b, c: (b * nchunk + c, 0, col // w))
    whole = lambda *shape: pl.BlockSpec(shape, lambda b, c: (0,) * len(shape))
    st_spec = pl.BlockSpec((1, B_HEADS, B_DV, B_DK), lambda b, c: (b, 0, 0, 0))
    return pl.pallas_call(
        functools.partial(_gla_kernel, chunk=chunk, sub=sub),
        grid=(batch, nchunk),
        in_specs=[blk(kw, COL_BQ), blk(kw, COL_BK), blk(vw, COL_BV), blk(LANE, COL_LR), blk(vw, COL_BR), st_spec,
                  whole(LANE, kw), whole(1, kw), whole(1, B_DV)],
        out_specs=[pl.BlockSpec((1, chunk, vw), lambda b, c: (b * nchunk + c, 0, 0)), st_spec],
        out_shape=[jax.ShapeDtypeStruct((batch * nchunk, chunk, vw), F32),
                   jax.ShapeDtypeStruct((batch, B_HEADS, B_DV, B_DK), F32)],
        scratch_shapes=[pltpu.VMEM((B_HEADS, B_DV, B_DK), F32)],
        compiler_params=_params("arbitrary", "arbitrary"),
        name="gla",
    )(proj3, proj3, proj3, proj3, proj3, state_t, w2p, gbias.reshape(1, kw), gn.reshape(1, B_DV))


def _xattn_kernel(q_ref, k_ref, v_ref, gq_ref, gk_ref, o_ref, *rest, bb, norm_k):
    gq, gk = gq_ref[...], gk_ref[...]
    scale = C_HD ** -0.5
    for i in range(bb):
        q, mk, mv = q_ref[i], k_ref[i], v_ref[i]
        outs, kns = [], []
        for h in range(C_HEADS):
            hs = slice(C_HD * h, C_HD * (h + 1))
            kh = mk[:, hs]
            if norm_k:
                kh = _rms(kh, gk)
                kns.append(kh)
            qh = _rms(q[:, hs], gq)
            s = lax.dot_general(qh.astype(BF16), kh.astype(BF16), NT_DIMS, preferred_element_type=F32) * scale
            p = jnp.exp(s - jnp.max(s, axis=-1, keepdims=True))
            p = p / jnp.sum(p, axis=-1, keepdims=True)
            outs.append(jnp.dot(p.astype(BF16), mv[:, hs].astype(BF16), preferred_element_type=F32))
        o_ref[i] = jnp.concatenate(outs, axis=-1)
        if norm_k:
            rest[0][i] = jnp.concatenate(kns, axis=-1)


def xattn(proj3, mem_k, mem_v_arr, v_col, gq, gk, bb, tq, norm_k):
    batch, t, _ = proj3.shape
    cw = C_HEADS * C_HD
    whole = lambda *shape: pl.BlockSpec(shape, lambda b, i: (0,) * len(shape))
    out_specs = [pl.BlockSpec((bb, tq, cw), lambda b, i: (b, i, 0))]
    out_shape = [jax.ShapeDtypeStruct((batch, t, cw), F32)]
    if norm_k:
        out_specs.append(pl.BlockSpec((bb, N_MEM, cw), lambda b, i: (b, 0, 0)))
        out_shape.append(jax.ShapeDtypeStruct((batch, N_MEM, cw), F32))
    return pl.pallas_call(
        functools.partial(_xattn_kernel, bb=bb, norm_k=norm_k),
        grid=(batch // bb, t // tq),
        in_specs=[pl.BlockSpec((bb, tq, cw), lambda b, i: (b, i, COL_CQ // cw)),
                  pl.BlockSpec((bb, N_MEM, cw), lambda b, i: (b, 0, 0)),
                  pl.BlockSpec((bb, N_MEM, cw), lambda b, i: (b, 0, v_col)),
                  whole(1, C_HD), whole(1, C_HD)],
        out_specs=out_specs,
        out_shape=out_shape,
        compiler_params=_params("arbitrary", "arbitrary"),
        name="xattn",
    )(proj3, mem_k, mem_v_arr, gq.reshape(1, C_HD), gk.reshape(1, C_HD))


def _merge_kernel(oa_ref, ob_ref, oc_ref, gl_ref, bg_ref, x_ref, wa_ref, wb_ref, wc_ref, wo_ref, g2_ref,
                  h_ref, xn_ref):
    d = D_MODEL
    gate = jax.nn.sigmoid(gl_ref[...] + bg_ref[...])
    pa = jnp.dot(oa_ref[...].astype(BF16), wa_ref[...], preferred_element_type=F32)
    pb = jnp.dot(ob_ref[...].astype(BF16), wb_ref[...], preferred_element_type=F32)
    pc = jnp.dot(oc_ref[...].astype(BF16), wc_ref[...], preferred_element_type=F32)
    merged = gate[:, :d] * pa + gate[:, d:2 * d] * pb + gate[:, 2 * d:] * pc
    h = x_ref[...] + jnp.dot(merged.astype(BF16), wo_ref[...], preferred_element_type=F32)
    h_ref[...] = h
    xn_ref[...] = _rms(h, g2_ref[...]).astype(BF16)


def merge(oa, ob, oc, proj, b_gates, x, wa, wb, wc, wo, g2, tm):
    n, d = x.shape
    bw = oa.shape[1]
    tok = lambda w: pl.BlockSpec((tm, w), lambda i: (i, 0))
    whole = lambda *shape: pl.BlockSpec(shape, lambda i: (0,) * len(shape))
    return pl.pallas_call(
        _merge_kernel,
        grid=(n // tm,),
        in_specs=[tok(bw), tok(bw), tok(bw), tok(3 * d), whole(1, 3 * d), tok(d),
                  whole(bw, d), whole(bw, d), whole(bw, d), whole(d, d), whole(1, d)],
        out_specs=[tok(d), tok(d)],
        out_shape=[jax.ShapeDtypeStruct((n, d), F32), jax.ShapeDtypeStruct((n, d), BF16)],
        compiler_params=_params("arbitrary"),
        name="merge",
    )(oa, ob, oc, proj, b_gates.reshape(1, 3 * d), x, wa, wb, wc, wo, g2.reshape(1, d))


_UNRANKED = float(P_NKEYS)


def _top16_exact(s):
    nrow = s.shape[0]
    row = lax.broadcasted_iota(I32, s.shape, 0)
    rank = jnp.full(s.shape, _UNRANKED, F32)
    vals = []
    for kk in range(P_TOPK):
        m = jnp.max(s, axis=0, keepdims=True)
        idx = jnp.min(jnp.where(s == m, row, nrow), axis=0, keepdims=True)
        hit = row == idx
        rank = jnp.where(hit, float(kk), rank)
        s = jnp.where(hit, -jnp.inf, s)
        vals.append(m)
    return jnp.concatenate(vals, axis=0), rank


def _top16_distinct(s, want_rank=True):
    rank = jnp.zeros(s.shape, F32)
    m = jnp.max(s, axis=0, keepdims=True)
    vals = [m]
    for kk in range(1, P_TOPK):
        below = s < m
        if want_rank:
            rank = jnp.where(below, float(kk), rank)
        m = jnp.max(jnp.where(below, s, -jnp.inf), axis=0, keepdims=True)
        vals.append(m)
    below = s < m
    rank = jnp.where(below, _UNRANKED, rank)
    count = jnp.sum(jnp.where(below, 0.0, 1.0), axis=0, keepdims=True)
    return jnp.concatenate(vals, axis=0), rank, below, count == float(P_TOPK)


_CAND_ROWS = 16 + 7 * 8 + 8


def _candidates(v1, v2):
    blocks = [v1[0:1] + v2[0:16]]
    blocks += [v1[a:a + 1] + v2[0:8] for a in range(1, 8)]
    blocks.append(v1[8:16] + v2[0:1])
    return jnp.concatenate(blocks, axis=0)


def _route_store(refs, ls, s1, s2, v1, rank1, v2, rank2, cand, sel):
    nb_ref, e1_ref, r2_ref, e2_ref = refs
    cmax = v1[0:1] + v2[0:1]
    zsum = jnp.sum(jnp.where(sel, jnp.exp(cand - cmax), 0.0), axis=0, keepdims=True)
    self32 = jnp.where(sel, 1.0, 0.0)
    nb = jnp.zeros(s1.shape, F32)
    nb = jnp.where(rank1 == 0.0, jnp.sum(self32[0:16], axis=0, keepdims=True), nb)
    for a in range(1, 8):
        cnt = jnp.sum(self32[16 + 8 * (a - 1):16 + 8 * a], axis=0, keepdims=True)
        nb = jnp.where(rank1 == float(a), cnt, nb)
    for a in range(8, 16):
        nb = jnp.where(rank1 == float(a), self32[72 + a - 8:73 + a - 8], nb)
    nb_ref[0, :, ls] = nb
    e1_ref[0, :, ls] = jnp.where(rank1 < float(P_TOPK), jnp.exp(s1 - v1[0:1]), 0.0)
    r2_ref[0, :, ls] = rank2.astype(BF16)
    e2_ref[0, :, ls] = jnp.where(rank2 < float(P_TOPK), jnp.exp(s2 - v2[0:1]) / zsum, 0.0).astype(BF16)


def _peer_route_kernel(xn_ref, wq_ref, k1_ref, k2_ref, nb_ref, e1_ref, r2_ref, e2_ref, s_ref):
    tb = xn_ref.shape[0]
    refs = (nb_ref, e1_ref, r2_ref, e2_ref)
    qt = lax.dot_general(wq_ref[...], xn_ref[...], NT_DIMS, preferred_element_type=F32)
    half = qt.shape[0] // 2
    s_ref[0] = jnp.dot(k1_ref[...], qt[:half].astype(BF16), preferred_element_type=F32)
    s_ref[1] = jnp.dot(k2_ref[...], qt[half:].astype(BF16), preferred_element_type=F32)
    for st in range(tb // LANE):
        ls = slice(LANE * st, LANE * (st + 1))
        s1 = s_ref[0, :, ls]
        s2 = s_ref[1, :, ls]
        v1, rank1, _, ok1 = _top16_distinct(s1)
        v2, rank2, _, ok2 = _top16_distinct(s2)
        cand = _candidates(v1, v2)
        _, _, cbelow, okc = _top16_distinct(cand, want_rank=False)
        _route_store(refs, ls, s1, s2, v1, rank1, v2, rank2, cand, jnp.logical_not(cbelow))
        ok = jnp.logical_and(jnp.logical_and(ok1, ok2), okc)
        all_ok = jnp.min(jnp.where(ok, 1.0, 0.0)) > 0.5

        @pl.when(jnp.logical_not(all_ok))
        def _(ls=ls):
            s1 = s_ref[0, :, ls]
            s2 = s_ref[1, :, ls]
            v1, rank1 = _top16_exact(s1)
            v2, rank2 = _top16_exact(s2)
            cand = _candidates(v1, v2)
            _, crank = _top16_exact(cand)
            _route_store(refs, ls, s1, s2, v1, rank1, v2, rank2, cand, crank < float(P_TOPK))


def peer_route(xn, wq_t, k1, k2, tb):
    n, d = xn.shape
    hw = wq_t.shape[0] // P_HEADS
    ospec = pl.BlockSpec((1, P_NKEYS, tb), lambda i, h: (h, 0, i))
    shape = lambda dt: jax.ShapeDtypeStruct((P_HEADS, P_NKEYS, n), dt)
    return pl.pallas_call(
        _peer_route_kernel,
        grid=(n // tb, P_HEADS),
        in_specs=[pl.BlockSpec((tb, d), lambda i, h: (i, 0)),
                  pl.BlockSpec((hw, d), lambda i, h: (h, 0)),
                  pl.BlockSpec((P_NKEYS, hw // 2), lambda i, h: (0, 0)),
                  pl.BlockSpec((P_NKEYS, hw // 2), lambda i, h: (0, 0))],
        out_specs=[ospec] * 4,
        out_shape=[shape(F32), shape(F32), shape(BF16), shape(BF16)],
        scratch_shapes=[pltpu.VMEM((2, P_NKEYS, tb), F32)],
        compiler_params=_params("arbitrary", "arbitrary"),
        name="peer_route",
    )(xn, wq_t, k1, k2)


MXU_DIM = 256


def _peer_dense_kernel(xn_ref, u_ref, vt_ref, nb_ref, e1_ref, r2_ref, e2_ref, h_ref, y_ref, acc_ref):
    j = pl.program_id(1)
    tb = xn_ref.shape[0]
    eb = u_ref.shape[0]

    @pl.when(j == 0)
    def _():
        acc_ref[...] = jnp.zeros_like(acc_ref)

    n1 = eb // P_NKEYS
    assert n1 == 8
    grp = pl.ds(pl.multiple_of(j * n1, n1), n1)
    per_chunk = MXU_DIM // P_NKEYS
    zero = jnp.zeros((), BF16)
    total = None
    for cc in range(eb // MXU_DIM):
        es = slice(MXU_DIM * cc, MXU_DIM * (cc + 1))
        hid = lax.dot_general(u_ref[es, :], xn_ref[...], NT_DIMS, preferred_element_type=F32)
        rows = []
        for c2 in range(per_chunk):
            c = per_chunk * cc + c2
            tiles = []
            for st in range(tb // LANE):
                ls = slice(LANE * st, LANE * (st + 1))
                gsum = jnp.zeros((P_NKEYS, LANE), BF16)
                for h in range(P_HEADS):
                    nbrow = nb_ref[h, grp, ls][c:c + 1].astype(BF16)
                    e1row = e1_ref[h, grp, ls][c:c + 1].astype(BF16)
                    gsum = gsum + jnp.where(r2_ref[h, :, ls] < nbrow, e2_ref[h, :, ls], zero) * e1row
                hh = hid[P_NKEYS * c2:P_NKEYS * (c2 + 1), ls]
                gelu = 0.5 * hh * (1.0 + lax.erf(hh * math.sqrt(0.5)))
                tiles.append(gelu.astype(BF16) * gsum)
            rows.append(jnp.concatenate(tiles, axis=1))
        act = jnp.concatenate(rows, axis=0)
        part = jnp.dot(vt_ref[:, es], act, preferred_element_type=F32)
        total = part if total is None else total + part
    acc_ref[...] += total

    @pl.when(j == pl.num_programs(1) - 1)
    def _():
        y_ref[...] = h_ref[...] + acc_ref[...].T


def peer_dense(xn, u, vt, nb, e1, r2, e2, h, tb, eb):
    n, d = xn.shape
    ne = u.shape[0]
    rspec = pl.BlockSpec((P_HEADS, P_NKEYS, tb), lambda i, j: (0, 0, i))
    return pl.pallas_call(
        _peer_dense_kernel,
        grid=(n // tb, ne // eb),
        in_specs=[pl.BlockSpec((tb, d), lambda i, j: (i, 0)),
                  pl.BlockSpec((eb, d), lambda i, j: (j, 0)),
                  pl.BlockSpec((d, eb), lambda i, j: (0, j)),
                  rspec, rspec, rspec, rspec,
                  pl.BlockSpec((tb, d), lambda i, j: (i, 0))],
        out_specs=pl.BlockSpec((tb, d), lambda i, j: (i, 0)),
        out_shape=jax.ShapeDtypeStruct((n, d), F32),
        scratch_shapes=[pltpu.VMEM((d, tb), F32)],
        compiler_params=_params("arbitrary", "arbitrary"),
        name="peer_dense",
    )(xn, u, vt, nb, e1, r2, e2, h)


TOKEN_TILE = 512
PEER_EXPERT_TILE = 8 * P_NKEYS
SAMPLE_BATCH_TILE = 8


def _permute_w_in(w_in):
    d = w_in.shape[0]
    seg = lambda a, b: w_in[:, a:b]
    lr = jnp.pad(seg(1792, 1808), ((0, 0), (0, LANE - B_RANK)))
    cols = [seg(2832, 5904), seg(0, 512), seg(1280, 1792), seg(1808, 2320), seg(2320, 2832), seg(768, 1024),
            seg(1024, 1280), seg(512, 640), seg(640, 768), lr, jnp.zeros((d, IN_COLS - COL_LR - LANE), w_in.dtype)]
    return jnp.concatenate(cols, axis=1).astype(BF16)


def kernel(x_prompt, x_sample, mem_prompt, cache_win_k, cache_win_v, state_gla, cache_mem_k, cache_mem_v, norm1_g, w_in, b_gates, a_qnorm_g, a_knorm_g, a_sinks, rel_bias, b_gate_w2, b_gate_b, b_gn_g, c_qnorm_g, c_knorm_g, mem_norm_g, w_mem_kv, w_br_a, w_br_b, w_br_c, w_out, norm2_g, peer_wq, peer_k1, peer_k2, peer_u, peer_v):
    depth = w_in.shape[0]
    assert depth == 1, "one layer"
    l = 0
    bp, tp, d = x_prompt.shape
    bs, ts, _ = x_sample.shape
    cw = C_HEADS * C_HD
    kvw = A_KV_HEADS * A_HD

    w_all = _permute_w_in(w_in[l])
    w2p = jnp.pad(b_gate_w2[l], ((0, LANE - B_RANK), (0, 0))).astype(BF16)
    wa, wb, wc, wo = (w.astype(BF16) for w in (w_br_a[l], w_br_b[l], w_br_c[l], w_out[l]))
    wq_t = peer_wq[l].T.astype(BF16)
    k1, k2 = peer_k1[l].astype(BF16), peer_k2[l].astype(BF16)
    u = peer_u[l].astype(BF16)
    vt = peer_v[l].T.astype(BF16)

    def channel_mix(oa, ob, oc, proj, x2):
        tm = min(TOKEN_TILE, x2.shape[0])
        h, xn2 = merge(oa, ob, oc, proj, b_gates[l], x2, wa, wb, wc, wo, norm2_g[l], tm)
        nb, e1, r2, e2 = peer_route(xn2, wq_t, k1, k2, tm)
        return peer_dense(xn2, u, vt, nb, e1, r2, e2, h, tm, PEER_EXPERT_TILE)

    xp = x_prompt.reshape(bp * tp, d)
    proj_p = norm_matmul(xp, norm1_g[l], w_all, min(TOKEN_TILE, bp * tp), IN_COLS // 2)
    memkv = norm_matmul(mem_prompt.reshape(bp * N_MEM, d), mem_norm_g[l], w_mem_kv[l].astype(BF16), min(TOKEN_TILE, bp * N_MEM), 2 * cw)
    oa_p, wk_p, wv_p = swa_prompt(proj_p, bp, tp, rel_bias, a_qnorm_g[l], a_knorm_g[l], a_sinks[l])
    nchunk = tp // B_CHUNK
    ob_p, st_p = gla(proj_p.reshape(bp * nchunk, B_CHUNK, IN_COLS), jnp.zeros((bp, B_HEADS, B_DV, B_DK), F32),
                     bp, nchunk, w2p, b_gate_b[l], b_gn_g[l])
    memkv3 = memkv.reshape(bp, N_MEM, 2 * cw)
    oc_p, mk_p = xattn(proj_p.reshape(bp, tp, IN_COLS), memkv3, memkv3, 1, c_qnorm_g[l], c_knorm_g[l],
                       1, min(TOKEN_TILE, tp), True)
    y_p = channel_mix(oa_p, ob_p.reshape(bp * tp, -1), oc_p.reshape(bp * tp, cw), proj_p, xp)

    xs = x_sample.reshape(bs * ts, d)
    proj_s = norm_matmul(xs, norm1_g[l], w_all, min(TOKEN_TILE, bs * ts), IN_COLS // 2)
    proj_s3 = proj_s.reshape(bs, ts, IN_COLS)
    ck = cache_win_k[l].reshape(bs, -1, kvw)
    cv = cache_win_v[l].reshape(bs, -1, kvw)
    oa_s, knew = swa_sample(proj_s3, ck, cv, rel_bias, a_qnorm_g[l], a_knorm_g[l], a_sinks[l], SAMPLE_BATCH_TILE)
    vnew = proj_s3[:, :, COL_AV:COL_AV + kvw]
    ob_s, st_s = gla(proj_s3, jnp.swapaxes(state_gla[l], -1, -2), bs, 1, w2p, b_gate_b[l], b_gn_g[l])
    oc_s, = xattn(proj_s3, cache_mem_k[l].reshape(bs, N_MEM, cw), cache_mem_v[l].reshape(bs, N_MEM, cw), 0,
                  c_qnorm_g[l], c_knorm_g[l], SAMPLE_BATCH_TILE, ts, False)
    y_s = channel_mix(oa_s.reshape(bs * ts, -1), ob_s.reshape(bs * ts, -1), oc_s.reshape(bs * ts, cw), proj_s, xs)

    wc_len = ck.shape[1]
    new_wk_s = jnp.concatenate([ck, knew], axis=1)[:, -wc_len:]
    new_wv_s = jnp.concatenate([cv, vnew], axis=1)[:, -wc_len:]
    return (y_p.reshape(bp, tp, d), y_s.reshape(bs, ts, d),
            wk_p.reshape(1, bp, WINDOW, A_KV_HEADS, A_HD), wv_p.reshape(1, bp, WINDOW, A_KV_HEADS, A_HD),
            jnp.swapaxes(st_p, -1, -2)[None],
            mk_p.reshape(1, bp, N_MEM, C_HEADS, C_HD), memkv3[:, :, cw:].reshape(1, bp, N_MEM, C_HEADS, C_HD),
            new_wk_s.reshape(1, bs, wc_len, A_KV_HEADS, A_HD), new_wv_s.reshape(1, bs, wc_len, A_KV_HEADS, A_HD),
            jnp.swapaxes(st_s, -1, -2)[None])
```

```python
import functools
import math

import numpy as np
import jax
import jax.numpy as jnp
from jax import lax
from jax.experimental import pallas as pl
from jax.experimental.pallas import tpu as pltpu

F32 = jnp.float32
BF16 = jnp.bfloat16
I32 = jnp.int32

EPS = 1e-6
D_MODEL = 1024
N_MEM = 256
A_HEADS, A_KV_HEADS, A_GROUP, A_HD = 8, 2, 4, 64
WINDOW = 128
N_BUCKETS, MAX_DISTANCE = 32, 128
B_HEADS, B_DK, B_DV, B_RANK, B_TAU = 4, 64, 128, 16, 16.0
B_CHUNK, B_SUB = 64, 16
C_HEADS, C_HD = 4, 128
P_HEADS, P_NKEYS, P_TOPK = 8, 128, 16
P_EXPERTS = P_NKEYS * P_NKEYS

LANE = 128
COL_GL, COL_AQ, COL_BV, COL_BR, COL_CQ, COL_BQ, COL_BK, COL_AK, COL_AV, COL_LR = (
    0, 3072, 3584, 4096, 4608, 5120, 5376, 5632, 5760, 5888)
IN_COLS = 6144
VMEM_LIMIT = 56 * 1024 * 1024

NT_DIMS = (((1,), (1,)), ((), ()))
TN_DIMS = (((0,), (0,)), ((), ()))


def _rms(x, g):
    ms = jnp.mean(x * x, axis=-1, keepdims=True)
    return x * lax.rsqrt(ms + EPS) * g


def _params(*sem, flags=None):
    return pltpu.CompilerParams(dimension_semantics=sem, vmem_limit_bytes=VMEM_LIMIT, flags=flags)


def _norm_matmul_kernel(x_ref, g_ref, w_ref, o_ref):
    xn = _rms(x_ref[...], g_ref[...]).astype(BF16)
    o_ref[...] = jnp.dot(xn, w_ref[...], preferred_element_type=F32)


def norm_matmul(x, g, w, tm, tn):
    n, d = x.shape
    m = w.shape[1]
    return pl.pallas_call(
        _norm_matmul_kernel,
        grid=(m // tn, n // tm),
        in_specs=[pl.BlockSpec((tm, d), lambda j, i: (i, 0)),
                  pl.BlockSpec((1, d), lambda j, i: (0, 0)),
                  pl.BlockSpec((d, tn), lambda j, i: (0, j))],
        out_specs=pl.BlockSpec((tm, tn), lambda j, i: (i, j)),
        out_shape=jax.ShapeDtypeStruct((n, m), F32),
        compiler_params=_params("arbitrary", "arbitrary"),
        name="norm_matmul",
    )(x, g.reshape(1, d), w)


def _t5_bucket_np(dist):
    n = np.maximum(dist, 0)
    max_exact = N_BUCKETS // 2
    nf = np.maximum(n, 1).astype(np.float32)
    large = max_exact + (np.log(nf / np.float32(max_exact)) / np.float32(math.log(MAX_DISTANCE / max_exact))
                         * np.float32(N_BUCKETS - max_exact)).astype(np.int32)
    large = np.minimum(large, N_BUCKETS - 1)
    return np.where(n < max_exact, n, large).astype(np.int32)


def _bias_from_buckets(bucket, rb_ref, head):
    tab = jnp.zeros(bucket.shape, F32)
    for bk in range(N_BUCKETS):
        tab = jnp.where(bucket == bk, rb_ref[bk, head], tab)
    return tab


def _swa_prompt_kernel(rb_ref, q_ref, kc_ref, kp_ref, vc_ref, vp_ref, gq_ref, gk_ref, snk_ref, bucket_ref,
                       o_ref, wk_ref, wv_ref, bias_ref):
    blk = WINDOW
    first = jnp.logical_and(pl.program_id(0) == 0, pl.program_id(1) == 0)

    @pl.when(first)
    def _():
        bucket = bucket_ref[...]
        qi = lax.broadcasted_iota(I32, (blk, 2 * blk), 0)
        kj = lax.broadcasted_iota(I32, (blk, 2 * blk), 1)
        dist = qi + blk - kj
        band = jnp.logical_and(dist >= 0, dist < WINDOW)
        for h in range(A_HEADS):
            bias_ref[h] = jnp.where(band, _bias_from_buckets(bucket, rb_ref, h), -jnp.inf)

    i = pl.program_id(1)
    q = q_ref[...]
    kc, kp, vc, vp = kc_ref[...], kp_ref[...], vc_ref[...], vp_ref[...]
    gq, gk = gq_ref[...], gk_ref[...]
    col = lax.broadcasted_iota(I32, (A_GROUP * blk, 2 * blk), 1)
    keep = jnp.logical_or(col >= blk, i > 0)
    scale = A_HD ** -0.5
    outs = [None] * A_HEADS
    kn_out = []
    for g in range(A_KV_HEADS):
        ks = slice(A_HD * g, A_HD * (g + 1))
        kcn = _rms(kc[:, ks], gk)
        kpn = _rms(kp[:, ks], gk)
        kn_out.append(kcn)
        kcat = jnp.concatenate([kpn, kcn], axis=0).astype(BF16)
        vcat = jnp.concatenate([vp[:, ks], vc[:, ks]], axis=0).astype(BF16)
        qg = jnp.concatenate(
            [_rms(q[:, A_HD * (A_GROUP * g + j):A_HD * (A_GROUP * g + j + 1)], gq) for j in range(A_GROUP)], axis=0)
        s = lax.dot_general(qg.astype(BF16), kcat, NT_DIMS, preferred_element_type=F32) * scale
        s = s + bias_ref[A_GROUP * g:A_GROUP * (g + 1)].reshape(A_GROUP * blk, 2 * blk)
        s = jnp.where(keep, s, -jnp.inf)
        snk = snk_ref[g]
        m = jnp.maximum(jnp.max(s, axis=-1, keepdims=True), snk)
        p = jnp.exp(s - m)
        den = jnp.sum(p, axis=-1, keepdims=True) + jnp.exp(snk - m)
        o = jnp.dot((p / den).astype(BF16), vcat, preferred_element_type=F32)
        for j in range(A_GROUP):
            outs[A_GROUP * g + j] = o[blk * j:blk * (j + 1)]
    o_ref[...] = jnp.concatenate(outs, axis=-1)
    wk_ref[0] = jnp.concatenate(kn_out, axis=-1)
    wv_ref[0] = vc


def swa_prompt(proj, batch, seq, rel_bias, gq, gk, sinks):
    blk = WINDOW
    nb = seq // blk
    qi = np.arange(blk)[:, None]
    kj = np.arange(2 * blk)[None, :]
    bucket = jnp.asarray(_t5_bucket_np(qi + blk - kj))
    snk = jnp.broadcast_to(sinks.astype(F32).reshape(A_KV_HEADS, A_GROUP, 1, 1),
                           (A_KV_HEADS, A_GROUP, blk, 1)).reshape(A_KV_HEADS, A_GROUP * blk, 1)
    cur = lambda c: (lambda b, i: (b * nb + i, c))
    prev = lambda c: (lambda b, i: (b * nb + jnp.maximum(i - 1, 0), c))
    kvw = A_KV_HEADS * A_HD
    return pl.pallas_call(
        _swa_prompt_kernel,
        grid=(batch, nb),
        in_specs=[pl.BlockSpec(memory_space=pltpu.SMEM),
                  pl.BlockSpec((blk, A_HEADS * A_HD), cur(COL_AQ // (A_HEADS * A_HD))),
                  pl.BlockSpec((blk, kvw), cur(COL_AK // kvw)),
                  pl.BlockSpec((blk, kvw), prev(COL_AK // kvw)),
                  pl.BlockSpec((blk, kvw), cur(COL_AV // kvw)),
                  pl.BlockSpec((blk, kvw), prev(COL_AV // kvw)),
                  pl.BlockSpec((1, A_HD), lambda b, i: (0, 0)),
                  pl.BlockSpec((1, A_HD), lambda b, i: (0, 0)),
                  pl.BlockSpec((A_KV_HEADS, A_GROUP * blk, 1), lambda b, i: (0, 0, 0)),
                  pl.BlockSpec((blk, 2 * blk), lambda b, i: (0, 0))],
        out_specs=[pl.BlockSpec((blk, A_HEADS * A_HD), lambda b, i: (b * nb + i, 0)),
                   pl.BlockSpec((1, blk, kvw), lambda b, i: (b, 0, 0)),
                   pl.BlockSpec((1, blk, kvw), lambda b, i: (b, 0, 0))],
        out_shape=[jax.ShapeDtypeStruct((batch * seq, A_HEADS * A_HD), F32),
                   jax.ShapeDtypeStruct((batch, blk, kvw), F32),
                   jax.ShapeDtypeStruct((batch, blk, kvw), F32)],
        scratch_shapes=[pltpu.VMEM((A_HEADS, blk, 2 * blk), F32)],
        compiler_params=_params("arbitrary", "arbitrary"),
        name="swa_prompt",
    )(rel_bias.astype(F32), proj, proj, proj, proj, proj, gq.reshape(1, A_HD), gk.reshape(1, A_HD), snk, bucket)


def _swa_sample_kernel(rb_ref, snk_ref, q_ref, kn_ref, vn_ref, ck_ref, cv_ref, gq_ref, gk_ref, bkt_c_ref, bkt_n_ref,
                       o_ref, knew_ref, bias_c_ref, bias_n_ref, *, t_new):
    wc = WINDOW

    @pl.when(pl.program_id(0) == 0)
    def _():
        t = lax.broadcasted_iota(I32, (t_new, LANE), 0)
        c = lax.broadcasted_iota(I32, (t_new, LANE), 1)
        valid_c = c > t
        valid_n = c <= t
        for h in range(A_HEADS):
            bias_c_ref[h] = jnp.where(valid_c, _bias_from_buckets(bkt_c_ref[...], rb_ref, h), -jnp.inf)
            bias_n_ref[h] = jnp.where(valid_n, _bias_from_buckets(bkt_n_ref[...], rb_ref, h), -jnp.inf)

    q = q_ref[...]
    gq, gk = gq_ref[...], gk_ref[...]
    scale = A_HD ** -0.5
    kn_all = []
    kv = []
    for g in range(A_KV_HEADS):
        ks = slice(A_HD * g, A_HD * (g + 1))
        knn = _rms(kn_ref[:, :, ks], gk)
        kn_all.append(knn)
        kv.append((ck_ref[:, :, ks].astype(BF16), cv_ref[:, :, ks].astype(BF16), knn, vn_ref[:, :, ks]))
    outs = []
    for h in range(A_HEADS):
        kc, vc, knn, vnn = kv[h // A_GROUP]
        qh = _rms(q[:, :, A_HD * h:A_HD * (h + 1)], gq)
        s_c = jnp.einsum('bqd,bkd->bqk', qh.astype(BF16), kc, preferred_element_type=F32) * scale + bias_c_ref[h]
        s_n = [jnp.sum(qh * knn[:, j:j + 1, :], axis=-1, keepdims=True) * scale + bias_n_ref[h][:, j:j + 1]
               for j in range(t_new)]
        snk = snk_ref[h]
        m = jnp.maximum(jnp.max(s_c, axis=-1, keepdims=True), snk)
        for j in range(t_new):
            m = jnp.maximum(m, s_n[j])
        p_c = jnp.exp(s_c - m)
        p_n = [jnp.exp(s_n[j] - m) for j in range(t_new)]
        den = jnp.sum(p_c, axis=-1, keepdims=True) + jnp.exp(snk - m)
        for j in range(t_new):
            den = den + p_n[j]
        o = jnp.einsum('bqk,bkd->bqd', (p_c / den).astype(BF16), vc, preferred_element_type=F32)
        for j in range(t_new):
            o = o + (p_n[j] / den) * vnn[:, j:j + 1, :]
        outs.append(o)
    o_ref[...] = jnp.concatenate(outs, axis=-1)
    knew_ref[...] = jnp.concatenate(kn_all, axis=-1)


def swa_sample(proj3, cache_k, cache_v, rel_bias, gq, gk, sinks, bb):
    batch, t_new, _ = proj3.shape
    wc = cache_k.shape[1]
    kvw = A_KV_HEADS * A_HD
    t = np.arange(t_new)[:, None]
    c = np.arange(LANE)[None, :]
    bkt_c = jnp.asarray(_t5_bucket_np(wc + t - c))
    bkt_n = jnp.asarray(_t5_bucket_np(t - c))
    whole = lambda *shape: pl.BlockSpec(shape, lambda i: (0,) * len(shape))
    return pl.pallas_call(
        functools.partial(_swa_sample_kernel, t_new=t_new),
        grid=(batch // bb,),
        in_specs=[pl.BlockSpec(memory_space=pltpu.SMEM),
                  pl.BlockSpec(memory_space=pltpu.SMEM),
                  pl.BlockSpec((bb, t_new, A_HEADS * A_HD), lambda i: (i, 0, COL_AQ // (A_HEADS * A_HD))),
                  pl.BlockSpec((bb, t_new, kvw), lambda i: (i, 0, COL_AK // kvw)),
                  pl.BlockSpec((bb, t_new, kvw), lambda i: (i, 0, COL_AV // kvw)),
                  pl.BlockSpec((bb, wc, kvw), lambda i: (i, 0, 0)),
                  pl.BlockSpec((bb, wc, kvw), lambda i: (i, 0, 0)),
                  whole(1, A_HD), whole(1, A_HD), whole(t_new, LANE), whole(t_new, LANE)],
        out_specs=[pl.BlockSpec((bb, t_new, A_HEADS * A_HD), lambda i: (i, 0, 0)),
                   pl.BlockSpec((bb, t_new, kvw), lambda i: (i, 0, 0))],
        out_shape=[jax.ShapeDtypeStruct((batch, t_new, A_HEADS * A_HD), F32),
                   jax.ShapeDtypeStruct((batch, t_new, kvw), F32)],
        scratch_shapes=[pltpu.VMEM((A_HEADS, t_new, LANE), F32), pltpu.VMEM((A_HEADS, t_new, LANE), F32)],
        compiler_params=_params("arbitrary"),
        name="swa_sample",
    )(rel_bias.astype(F32), sinks.astype(F32), proj3, proj3, proj3, cache_k, cache_v,
      gq.reshape(1, A_HD), gk.reshape(1, A_HD), bkt_c, bkt_n)


def _gla_kernel(q_ref, k_ref, v_ref, lr_ref, r_ref, s0_ref, w2_ref, gb_ref, gn_ref, o_ref, sout_ref, st_ref,
                *, chunk, sub):
    c = pl.program_id(1)

    @pl.when(c == 0)
    def _():
        st_ref[...] = s0_ref[0]

    z = jnp.dot(lr_ref[0].astype(BF16), w2_ref[...], preferred_element_type=F32) + gb_ref[...]
    g = (jnp.minimum(z, 0.0) - jnp.log1p(jnp.exp(-jnp.abs(z)))) / B_TAU
    width = g.shape[-1]
    row_w = lax.broadcasted_iota(I32, (chunk, width), 0)
    b = jnp.zeros_like(g)
    for s in range(chunk):
        b = b + jnp.where(row_w >= s, g[s:s + 1, :], 0.0)
    q = q_ref[0] * (B_DK ** -0.5)
    k, v, r = k_ref[0], v_ref[0], r_ref[0]
    gn = gn_ref[...]
    row = lax.broadcasted_iota(I32, (chunk, chunk), 0)
    col = lax.broadcasted_iota(I32, (chunk, chunk), 1)
    nsub = chunk // sub
    outs = []
    for h in range(B_HEADS):
        ks = slice(B_DK * h, B_DK * (h + 1))
        vs = slice(B_DV * h, B_DV * (h + 1))
        qh, kh, vh, bh = q[:, ks], k[:, ks], v[:, vs], b[:, ks]
        st = st_ref[h]
        o = lax.dot_general((qh * jnp.exp(bh)).astype(BF16), st.astype(BF16), NT_DIMS, preferred_element_type=F32)
        refs = [bh[sub * (j + 1) - 1:sub * (j + 1)] for j in range(nsub)]
        rfull = jnp.concatenate([jnp.broadcast_to(rj, (sub, B_DK)) for rj in refs], axis=0)
        kt = (kh * jnp.exp(rfull - bh)).astype(BF16)
        qcat = jnp.concatenate([qh * jnp.exp(jnp.minimum(bh - rj, 80.0)) for rj in refs], axis=0).astype(BF16)
        aa = lax.dot_general(qcat, kt, NT_DIMS, preferred_element_type=F32)
        att = jnp.zeros((chunk, chunk), F32)
        for j in range(nsub):
            in_sub = jnp.logical_and(col >= sub * j, col < sub * (j + 1))
            att = jnp.where(in_sub, aa[chunk * j:chunk * (j + 1)], att)
        att = jnp.where(col <= row, att, 0.0)
        o = o + jnp.dot(att.astype(BF16), vh.astype(BF16), preferred_element_type=F32)
        bl = bh[chunk - 1:chunk]
        kd = (kh * jnp.exp(bl - bh)).astype(BF16)
        st_ref[h] = st * jnp.exp(bl) + lax.dot_general(vh.astype(BF16), kd, TN_DIMS, preferred_element_type=F32)
        rh = r[:, vs]
        outs.append(_rms(o, gn) * (rh * jax.nn.sigmoid(rh)))
    o_ref[0] = jnp.concatenate(outs, axis=-1)
    sout_ref[0] = st_ref[...]


def gla(proj3, state_t, batch, nchunk, w2p, gbias, gn):
    chunk = proj3.shape[1]
    sub = min(B_SUB, chunk)
    kw, vw = B_HEADS * B_DK, B_HEADS * B_DV
    blk = lambda w, col: pl.BlockSpec((1, chunk, w), lambda b, c: (b * nchunk + c, 0, col // w))
    whole = lambda *shape: pl.BlockSpec(shape, lambda b, c: (0,) * len(shape))
    st_spec = pl.BlockSpec((1, B_HEADS, B_DV, B_DK), lambda b, c: (b, 0, 0, 0))
    return pl.pallas_call(
        functools.partial(_gla_kernel, chunk=chunk, sub=sub),
        grid=(batch, nchunk),
        in_specs=[blk(kw, COL_BQ), blk(kw, COL_BK), blk(vw, COL_BV), blk(LANE, COL_LR), blk(vw, COL_BR), st_spec,
                  whole(LANE, kw), whole(1, kw), whole(1, B_DV)],
        out_specs=[pl.BlockSpec((1, chunk, vw), lambda b, c: (b * nchunk + c, 0, 0)), st_spec],
        out_shape=[jax.ShapeDtypeStruct((batch * nchunk, chunk, vw), F32),
                   jax.ShapeDtypeStruct((batch, B_HEADS, B_DV, B_DK), F32)],
        scratch_shapes=[pltpu.VMEM((B_HEADS, B_DV, B_DK), F32)],
        compiler_params=_params("arbitrary", "arbitrary"),
        name="gla",
    )(proj3, proj3, proj3, proj3, proj3, state_t, w2p, gbias.reshape(1, kw), gn.reshape(1, B_DV))


def _xattn_kernel(q_ref, k_ref, v_ref, gq_ref, gk_ref, o_ref, *rest, bb, norm_k):
    gq, gk = gq_ref[...], gk_ref[...]
    scale = C_HD ** -0.5
    for i in range(bb):
        q, mk, mv = q_ref[i], k_ref[i], v_ref[i]
        outs, kns = [], []
        for h in range(C_HEADS):
            hs = slice(C_HD * h, C_HD * (h + 1))
            kh = mk[:, hs]
            if norm_k:
                kh = _rms(kh, gk)
                kns.append(kh)
            qh = _rms(q[:, hs], gq)
            s = lax.dot_general(qh.astype(BF16), kh.astype(BF16), NT_DIMS, preferred_element_type=F32) * scale
            p = jnp.exp(s - jnp.max(s, axis=-1, keepdims=True))
            p = p / jnp.sum(p, axis=-1, keepdims=True)
            outs.append(jnp.dot(p.astype(BF16), mv[:, hs].astype(BF16), preferred_element_type=F32))
        o_ref[i] = jnp.concatenate(outs, axis=-1)
        if norm_k:
            rest[0][i] = jnp.concatenate(kns, axis=-1)


def xattn(proj3, mem_k, mem_v_arr, v_col, gq, gk, bb, tq, norm_k):
    batch, t, _ = proj3.shape
    cw = C_HEADS * C_HD
    whole = lambda *shape: pl.BlockSpec(shape, lambda b, i: (0,) * len(shape))
    out_specs = [pl.BlockSpec((bb, tq, cw), lambda b, i: (b, i, 0))]
    out_shape = [jax.ShapeDtypeStruct((batch, t, cw), F32)]
    if norm_k:
        out_specs.append(pl.BlockSpec((bb, N_MEM, cw), lambda b, i: (b, 0, 0)))
        out_shape.append(jax.ShapeDtypeStruct((batch, N_MEM, cw), F32))
    return pl.pallas_call(
        functools.partial(_xattn_kernel, bb=bb, norm_k=norm_k),
        grid=(batch // bb, t // tq),
        in_specs=[pl.BlockSpec((bb, tq, cw), lambda b, i: (b, i, COL_CQ // cw)),
                  pl.BlockSpec((bb, N_MEM, cw), lambda b, i: (b, 0, 0)),
                  pl.BlockSpec((bb, N_MEM, cw), lambda b, i: (b, 0, v_col)),
                  whole(1, C_HD), whole(1, C_HD)],
        out_specs=out_specs,
        out_shape=out_shape,
        compiler_params=_params("arbitrary", "arbitrary"),
        name="xattn",
    )(proj3, mem_k, mem_v_arr, gq.reshape(1, C_HD), gk.reshape(1, C_HD))


def _merge_kernel(oa_ref, ob_ref, oc_ref, gl_ref, bg_ref, x_ref, wa_ref, wb_ref, wc_ref, wo_ref, g2_ref,
                  h_ref, xn_ref):
    d = D_MODEL
    gate = jax.nn.sigmoid(gl_ref[...] + bg_ref[...])
    pa = jnp.dot(oa_ref[...].astype(BF16), wa_ref[...], preferred_element_type=F32)
    pb = jnp.dot(ob_ref[...].astype(BF16), wb_ref[...], preferred_element_type=F32)
    pc = jnp.dot(oc_ref[...].astype(BF16), wc_ref[...], preferred_element_type=F32)
    merged = gate[:, :d] * pa + gate[:, d:2 * d] * pb + gate[:, 2 * d:] * pc
    h = x_ref[...] + jnp.dot(merged.astype(BF16), wo_ref[...], preferred_element_type=F32)
    h_ref[...] = h
    xn_ref[...] = _rms(h, g2_ref[...]).astype(BF16)


def merge(oa, ob, oc, proj, b_gates, x, wa, wb, wc, wo, g2, tm):
    n, d = x.shape
    bw = oa.shape[1]
    tok = lambda w: pl.BlockSpec((tm, w), lambda i: (i, 0))
    whole = lambda *shape: pl.BlockSpec(shape, lambda i: (0,) * len(shape))
    return pl.pallas_call(
        _merge_kernel,
        grid=(n // tm,),
        in_specs=[tok(bw), tok(bw), tok(bw), tok(3 * d), whole(1, 3 * d), tok(d),
                  whole(bw, d), whole(bw, d), whole(bw, d), whole(d, d), whole(1, d)],
        out_specs=[tok(d), tok(d)],
        out_shape=[jax.ShapeDtypeStruct((n, d), F32), jax.ShapeDtypeStruct((n, d), BF16)],
        compiler_params=_params("arbitrary"),
        name="merge",
    )(oa, ob, oc, proj, b_gates.reshape(1, 3 * d), x, wa, wb, wc, wo, g2.reshape(1, d))


_UNRANKED = float(P_NKEYS)


def _top16_exact(s):
    nrow = s.shape[0]
    row = lax.broadcasted_iota(I32, s.shape, 0)
    rank = jnp.full(s.shape, _UNRANKED, F32)
    vals = []
    for kk in range(P_TOPK):
        m = jnp.max(s, axis=0, keepdims=True)
        idx = jnp.min(jnp.where(s == m, row, nrow), axis=0, keepdims=True)
        hit = row == idx
        rank = jnp.where(hit, float(kk), rank)
        s = jnp.where(hit, -jnp.inf, s)
        vals.append(m)
    return jnp.concatenate(vals, axis=0), rank


def _top16_distinct(s, want_rank=True):
    rank = jnp.zeros(s.shape, F32)
    m = jnp.max(s, axis=0, keepdims=True)
    vals = [m]
    for kk in range(1, P_TOPK):
        below = s < m
        if want_rank:
            rank = jnp.where(below, float(kk), rank)
        m = jnp.max(jnp.where(below, s, -jnp.inf), axis=0, keepdims=True)
        vals.append(m)
    below = s < m
    rank = jnp.where(below, _UNRANKED, rank)
    count = jnp.sum(jnp.where(below, 0.0, 1.0), axis=0, keepdims=True)
    return jnp.concatenate(vals, axis=0), rank, below, count == float(P_TOPK)


_CAND_ROWS = 16 + 7 * 8 + 8


def _candidates(v1, v2):
    blocks = [v1[0:1] + v2[0:16]]
    blocks += [v1[a:a + 1] + v2[0:8] for a in range(1, 8)]
    blocks.append(v1[8:16] + v2[0:1])
    return jnp.concatenate(blocks, axis=0)


def _route_store(refs, ls, s1, s2, v1, rank1, v2, rank2, cand, sel):
    nb_ref, e1_ref, r2_ref, e2_ref = refs
    cmax = v1[0:1] + v2[0:1]
    zsum = jnp.sum(jnp.where(sel, jnp.exp(cand - cmax), 0.0), axis=0, keepdims=True)
    self32 = jnp.where(sel, 1.0, 0.0)
    nb = jnp.zeros(s1.shape, F32)
    nb = jnp.where(rank1 == 0.0, jnp.sum(self32[0:16], axis=0, keepdims=True), nb)
    for a in range(1, 8):
        cnt = jnp.sum(self32[16 + 8 * (a - 1):16 + 8 * a], axis=0, keepdims=True)
        nb = jnp.where(rank1 == float(a), cnt, nb)
    for a in range(8, 16):
        nb = jnp.where(rank1 == float(a), self32[72 + a - 8:73 + a - 8], nb)
    nb_ref[0, :, ls] = nb
    e1_ref[0, :, ls] = jnp.where(rank1 < float(P_TOPK), jnp.exp(s1 - v1[0:1]), 0.0)
    e2_ref[0, :, ls] = jnp.where(rank2 < float(P_TOPK), jnp.exp(s2 - v2[0:1]) / zsum, 0.0)
    r2_ref[0, :, ls] = pltpu.bitcast(rank2.astype(BF16), jnp.uint32)


def _peer_route_kernel(xn_ref, wq_ref, k1_ref, k2_ref, nb_ref, e1_ref, r2_ref, e2_ref, s_ref):
    tb = xn_ref.shape[0]
    refs = (nb_ref, e1_ref, r2_ref, e2_ref)
    qt = lax.dot_general(wq_ref[...], xn_ref[...], NT_DIMS, preferred_element_type=F32)
    half = qt.shape[0] // 2
    s_ref[0] = jnp.dot(k1_ref[...], qt[:half].astype(BF16), preferred_element_type=F32)
    s_ref[1] = jnp.dot(k2_ref[...], qt[half:].astype(BF16), preferred_element_type=F32)
    for st in range(tb // LANE):
        ls = slice(LANE * st, LANE * (st + 1))
        s1 = s_ref[0, :, ls]
        s2 = s_ref[1, :, ls]
        v1, rank1, _, ok1 = _top16_distinct(s1)
        v2, rank2, _, ok2 = _top16_distinct(s2)
        cand = _candidates(v1, v2)
        _, _, cbelow, okc = _top16_distinct(cand, want_rank=False)
        _route_store(refs, ls, s1, s2, v1, rank1, v2, rank2, cand, jnp.logical_not(cbelow))
        ok = jnp.logical_and(jnp.logical_and(ok1, ok2), okc)
        all_ok = jnp.min(jnp.where(ok, 1.0, 0.0)) > 0.5

        @pl.when(jnp.logical_not(all_ok))
        def _(ls=ls):
            s1 = s_ref[0, :, ls]
            s2 = s_ref[1, :, ls]
            v1, rank1 = _top16_exact(s1)
            v2, rank2 = _top16_exact(s2)
            cand = _candidates(v1, v2)
            _, crank = _top16_exact(cand)
            _route_store(refs, ls, s1, s2, v1, rank1, v2, rank2, cand, crank < float(P_TOPK))


def peer_route(xn, wq_t, k1, k2, tb):
    n, d = xn.shape
    hw = wq_t.shape[0] // P_HEADS
    ospec = pl.BlockSpec((1, P_NKEYS, tb), lambda i, h: (h, 0, i))
    pspec = pl.BlockSpec((1, P_NKEYS // 2, tb), lambda i, h: (h, 0, i))
    shape = lambda dt: jax.ShapeDtypeStruct((P_HEADS, P_NKEYS, n), dt)
    packed = jax.ShapeDtypeStruct((P_HEADS, P_NKEYS // 2, n), jnp.uint32)
    return pl.pallas_call(
        _peer_route_kernel,
        grid=(n // tb, P_HEADS),
        in_specs=[pl.BlockSpec((tb, d), lambda i, h: (i, 0)),
                  pl.BlockSpec((hw, d), lambda i, h: (h, 0)),
                  pl.BlockSpec((P_NKEYS, hw // 2), lambda i, h: (0, 0)),
                  pl.BlockSpec((P_NKEYS, hw // 2), lambda i, h: (0, 0))],
        out_specs=[ospec, ospec, pspec, ospec],
        out_shape=[shape(F32), shape(F32), packed, shape(F32)],
        scratch_shapes=[pltpu.VMEM((2, P_NKEYS, tb), F32)],
        compiler_params=_params("arbitrary", "arbitrary"),
        name="peer_route",
    )(xn, wq_t, k1, k2)


MXU_DIM = 256


BF16_ROWS = 16
C_GROUP = 2


def _peer_dense_kernel(xn_ref, u_ref, vt_ref, nb_ref, e1_ref, r2_ref, e2_ref, h_ref, y_ref,
                       acc_ref, hid_ref, act_ref):
    j = pl.program_id(1)
    tb = xn_ref.shape[0]
    eb = u_ref.shape[0]
    n1 = eb // P_NKEYS
    nrg = P_NKEYS // BF16_ROWS
    half = BF16_ROWS // 2

    @pl.when(j == 0)
    def _():
        acc_ref[...] = jnp.zeros_like(acc_ref)

    hid_ref[...] = lax.dot_general(u_ref[...], xn_ref[...], NT_DIMS, preferred_element_type=F32)
    zero = jnp.zeros((), BF16)

    def row_vreg(ref, c, h, ls):
        row = ref[pl.ds(h * P_NKEYS + j * n1 + c, 1), :]
        return jnp.broadcast_to(row[:, ls], (BF16_ROWS, LANE)).astype(BF16)

    for st in range(tb // LANE):
        ls = slice(LANE * st, LANE * (st + 1))
        for c0 in range(0, n1, C_GROUP):
            nbv = [[row_vreg(nb_ref, c0 + k, h, ls) for h in range(P_HEADS)] for k in range(C_GROUP)]
            e1v = [[row_vreg(e1_ref, c0 + k, h, ls) for h in range(P_HEADS)] for k in range(C_GROUP)]
            for rg in range(nrg):
                ws = slice(half * rg, half * (rg + 1))
                gs = [jnp.zeros((BF16_ROWS, LANE), BF16) for _ in range(C_GROUP)]
                for h in range(P_HEADS):
                    r2 = pltpu.bitcast(r2_ref[h, ws, ls], BF16)
                    e2 = e2_ref[h, BF16_ROWS * rg:BF16_ROWS * (rg + 1), ls].astype(BF16)
                    for k in range(C_GROUP):
                        gs[k] = gs[k] + jnp.where(r2 < nbv[k][h], e2, zero) * e1v[k][h]
                for k in range(C_GROUP):
                    rs = slice(P_NKEYS * (c0 + k) + BF16_ROWS * rg, P_NKEYS * (c0 + k) + BF16_ROWS * (rg + 1))
                    hh = hid_ref[rs, ls]
                    gelu = 0.5 * hh * (1.0 + lax.erf(hh * math.sqrt(0.5)))
                    act_ref[rs, ls] = gelu.astype(BF16) * gs[k]
    acc_ref[...] += jnp.dot(vt_ref[...], act_ref[...], preferred_element_type=F32)

    @pl.when(j == pl.num_programs(1) - 1)
    def _():
        y_ref[...] = h_ref[...] + acc_ref[...].T


def peer_dense(xn, u, vt, nb, e1, r2, e2, h, tb, eb):
    n, d = xn.shape
    ne = u.shape[0]
    rspec = pl.BlockSpec((P_HEADS * P_NKEYS, tb), lambda i, j: (0, i))
    pspec = pl.BlockSpec((P_HEADS, P_NKEYS // 2, tb), lambda i, j: (0, 0, i))
    nb = nb.reshape(P_HEADS * P_NKEYS, n)
    e1 = e1.reshape(P_HEADS * P_NKEYS, n)
    return pl.pallas_call(
        _peer_dense_kernel,
        grid=(n // tb, ne // eb),
        in_specs=[pl.BlockSpec((tb, d), lambda i, j: (i, 0)),
                  pl.BlockSpec((eb, d), lambda i, j: (j, 0)),
                  pl.BlockSpec((d, eb), lambda i, j: (0, j)),
                  rspec, rspec, pspec, pl.BlockSpec((P_HEADS, P_NKEYS, tb), lambda i, j: (0, 0, i)),
                  pl.BlockSpec((tb, d), lambda i, j: (i, 0))],
        out_specs=pl.BlockSpec((tb, d), lambda i, j: (i, 0)),
        out_shape=jax.ShapeDtypeStruct((n, d), F32),
        scratch_shapes=[pltpu.VMEM((d, tb), F32), pltpu.VMEM((eb, tb), F32), pltpu.VMEM((eb, tb), BF16)],
        compiler_params=_params("arbitrary", "arbitrary"),
        name="peer_dense",
    )(xn, u, vt, nb, e1, r2, e2, h)


TOKEN_TILE = 512
PEER_EXPERT_TILE = 8 * P_NKEYS
SAMPLE_BATCH_TILE = 8


def _permute_w_in(w_in):
    d = w_in.shape[0]
    seg = lambda a, b: w_in[:, a:b]
    lr = jnp.pad(seg(1792, 1808), ((0, 0), (0, LANE - B_RANK)))
    cols = [seg(2832, 5904), seg(0, 512), seg(1280, 1792), seg(1808, 2320), seg(2320, 2832), seg(768, 1024),
            seg(1024, 1280), seg(512, 640), seg(640, 768), lr, jnp.zeros((d, IN_COLS - COL_LR - LANE), w_in.dtype)]
    return jnp.concatenate(cols, axis=1).astype(BF16)


def kernel(x_prompt, x_sample, mem_prompt, cache_win_k, cache_win_v, state_gla, cache_mem_k, cache_mem_v, norm1_g, w_in, b_gates, a_qnorm_g, a_knorm_g, a_sinks, rel_bias, b_gate_w2, b_gate_b, b_gn_g, c_qnorm_g, c_knorm_g, mem_norm_g, w_mem_kv, w_br_a, w_br_b, w_br_c, w_out, norm2_g, peer_wq, peer_k1, peer_k2, peer_u, peer_v):
    depth = w_in.shape[0]
    assert depth == 1, "one layer"
    l = 0
    bp, tp, d = x_prompt.shape
    bs, ts, _ = x_sample.shape
    cw = C_HEADS * C_HD
    kvw = A_KV_HEADS * A_HD

    w_all = _permute_w_in(w_in[l])
    w2p = jnp.pad(b_gate_w2[l], ((0, LANE - B_RANK), (0, 0))).astype(BF16)
    wa, wb, wc, wo = (w.astype(BF16) for w in (w_br_a[l], w_br_b[l], w_br_c[l], w_out[l]))
    wq_t = peer_wq[l].T.astype(BF16)
    k1, k2 = peer_k1[l].astype(BF16), peer_k2[l].astype(BF16)
    u = peer_u[l].astype(BF16)
    vt = peer_v[l].T.astype(BF16)

    def channel_mix(oa, ob, oc, proj, x2):
        tm = min(TOKEN_TILE, x2.shape[0])
        h, xn2 = merge(oa, ob, oc, proj, b_gates[l], x2, wa, wb, wc, wo, norm2_g[l], tm)
        nb, e1, r2, e2 = peer_route(xn2, wq_t, k1, k2, tm)
        return peer_dense(xn2, u, vt, nb, e1, r2, e2, h, tm, PEER_EXPERT_TILE)

    xp = x_prompt.reshape(bp * tp, d)
    proj_p = norm_matmul(xp, norm1_g[l], w_all, min(TOKEN_TILE, bp * tp), IN_COLS // 2)
    memkv = norm_matmul(mem_prompt.reshape(bp * N_MEM, d), mem_norm_g[l], w_mem_kv[l].astype(BF16), min(TOKEN_TILE, bp * N_MEM), 2 * cw)
    oa_p, wk_p, wv_p = swa_prompt(proj_p, bp, tp, rel_bias, a_qnorm_g[l], a_knorm_g[l], a_sinks[l])
    nchunk = tp // B_CHUNK
    ob_p, st_p = gla(proj_p.reshape(bp * nchunk, B_CHUNK, IN_COLS), jnp.zeros((bp, B_HEADS, B_DV, B_DK), F32),
                     bp, nchunk, w2p, b_gate_b[l], b_gn_g[l])
    memkv3 = memkv.reshape(bp, N_MEM, 2 * cw)
    oc_p, mk_p = xattn(proj_p.reshape(bp, tp, IN_COLS), memkv3, memkv3, 1, c_qnorm_g[l], c_knorm_g[l],
                       1, min(TOKEN_TILE, tp), True)
    y_p = channel_mix(oa_p, ob_p.reshape(bp * tp, -1), oc_p.reshape(bp * tp, cw), proj_p, xp)

    xs = x_sample.reshape(bs * ts, d)
    proj_s = norm_matmul(xs, norm1_g[l], w_all, min(TOKEN_TILE, bs * ts), IN_COLS // 2)
    proj_s3 = proj_s.reshape(bs, ts, IN_COLS)
    ck = cache_win_k[l].reshape(bs, -1, kvw)
    cv = cache_win_v[l].reshape(bs, -1, kvw)
    oa_s, knew = swa_sample(proj_s3, ck, cv, rel_bias, a_qnorm_g[l], a_knorm_g[l], a_sinks[l], SAMPLE_BATCH_TILE)
    vnew = proj_s3[:, :, COL_AV:COL_AV + kvw]
    ob_s, st_s = gla(proj_s3, jnp.swapaxes(state_gla[l], -1, -2), bs, 1, w2p, b_gate_b[l], b_gn_g[l])
    oc_s, = xattn(proj_s3, cache_mem_k[l].reshape(bs, N_MEM, cw), cache_mem_v[l].reshape(bs, N_MEM, cw), 0,
                  c_qnorm_g[l], c_knorm_g[l], SAMPLE_BATCH_TILE, ts, False)
    y_s = channel_mix(oa_s.reshape(bs * ts, -1), ob_s.reshape(bs * ts, -1), oc_s.reshape(bs * ts, cw), proj_s, xs)

    wc_len = ck.shape[1]
    new_wk_s = jnp.concatenate([ck, knew], axis=1)[:, -wc_len:]
    new_wv_s = jnp.concatenate([cv, vnew], axis=1)[:, -wc_len:]
    return (y_p.reshape(bp, tp, d), y_s.reshape(bs, ts, d),
            wk_p.reshape(1, bp, WINDOW, A_KV_HEADS, A_HD), wv_p.reshape(1, bp, WINDOW, A_KV_HEADS, A_HD),
            jnp.swapaxes(st_p, -1, -2)[None],
            mk_p.reshape(1, bp, N_MEM, C_HEADS, C_HD), memkv3[:, :, cw:].reshape(1, bp, N_MEM, C_HEADS, C_HD),
            new_wk_s.reshape(1, bs, wc_len, A_KV_HEADS, A_HD), new_wv_s.reshape(1, bs, wc_len, A_KV_HEADS, A_HD),
            jnp.swapaxes(st_s, -1, -2)[None])
```

```python
import functools
import math

import numpy as np
import jax
import jax.numpy as jnp
from jax import lax
from jax.experimental import pallas as pl
from jax.experimental.pallas import tpu as pltpu

F32 = jnp.float32
BF16 = jnp.bfloat16
I32 = jnp.int32

EPS = 1e-6
D_MODEL = 1024
N_MEM = 256
A_HEADS, A_KV_HEADS, A_GROUP, A_HD = 8, 2, 4, 64
WINDOW = 128
N_BUCKETS, MAX_DISTANCE = 32, 128
B_HEADS, B_DK, B_DV, B_RANK, B_TAU = 4, 64, 128, 16, 16.0
B_CHUNK, B_SUB = 64, 16
C_HEADS, C_HD = 4, 128
P_HEADS, P_NKEYS, P_TOPK = 8, 128, 16
P_EXPERTS = P_NKEYS * P_NKEYS

LANE = 128
COL_GL, COL_AQ, COL_BV, COL_BR, COL_CQ, COL_BQ, COL_BK, COL_AK, COL_AV, COL_LR = (
    0, 3072, 3584, 4096, 4608, 5120, 5376, 5632, 5760, 5888)
IN_COLS = 6144
VMEM_LIMIT = 56 * 1024 * 1024

NT_DIMS = (((1,), (1,)), ((), ()))
TN_DIMS = (((0,), (0,)), ((), ()))


def _rms(x, g):
    ms = jnp.mean(x * x, axis=-1, keepdims=True)
    return x * lax.rsqrt(ms + EPS) * g


def _params(*sem, flags=None):
    return pltpu.CompilerParams(dimension_semantics=sem, vmem_limit_bytes=VMEM_LIMIT, flags=flags)


def _norm_matmul_kernel(x_ref, g_ref, w_ref, o_ref):
    xn = _rms(x_ref[...], g_ref[...]).astype(BF16)
    o_ref[...] = jnp.dot(xn, w_ref[...], preferred_element_type=F32)


def norm_matmul(x, g, w, tm, tn):
    n, d = x.shape
    m = w.shape[1]
    return pl.pallas_call(
        _norm_matmul_kernel,
        grid=(m // tn, n // tm),
        in_specs=[pl.BlockSpec((tm, d), lambda j, i: (i, 0)),
                  pl.BlockSpec((1, d), lambda j, i: (0, 0)),
                  pl.BlockSpec((d, tn), lambda j, i: (0, j))],
        out_specs=pl.BlockSpec((tm, tn), lambda j, i: (i, j)),
        out_shape=jax.ShapeDtypeStruct((n, m), F32),
        compiler_params=_params("arbitrary", "arbitrary"),
        name="norm_matmul",
    )(x, g.reshape(1, d), w)


def _t5_bucket_np(dist):
    n = np.maximum(dist, 0)
    max_exact = N_BUCKETS // 2
    nf = np.maximum(n, 1).astype(np.float32)
    large = max_exact + (np.log(nf / np.float32(max_exact)) / np.float32(math.log(MAX_DISTANCE / max_exact))
                         * np.float32(N_BUCKETS - max_exact)).astype(np.int32)
    large = np.minimum(large, N_BUCKETS - 1)
    return np.where(n < max_exact, n, large).astype(np.int32)


def _bias_from_buckets(bucket, rb_ref, head):
    tab = jnp.zeros(bucket.shape, F32)
    for bk in range(N_BUCKETS):
        tab = jnp.where(bucket == bk, rb_ref[bk, head], tab)
    return tab


def _swa_prompt_kernel(rb_ref, q_ref, kc_ref, kp_ref, vc_ref, vp_ref, gq_ref, gk_ref, snk_ref, bucket_ref,
                       o_ref, wk_ref, wv_ref, bias_ref):
    blk = WINDOW
    first = jnp.logical_and(pl.program_id(0) == 0, pl.program_id(1) == 0)

    @pl.when(first)
    def _():
        bucket = bucket_ref[...]
        qi = lax.broadcasted_iota(I32, (blk, 2 * blk), 0)
        kj = lax.broadcasted_iota(I32, (blk, 2 * blk), 1)
        dist = qi + blk - kj
        band = jnp.logical_and(dist >= 0, dist < WINDOW)
        for h in range(A_HEADS):
            bias_ref[h] = jnp.where(band, _bias_from_buckets(bucket, rb_ref, h), -jnp.inf)

    i = pl.program_id(1)
    nsub = q_ref.shape[0] // blk
    gq, gk = gq_ref[...], gk_ref[...]
    col = lax.broadcasted_iota(I32, (A_GROUP * blk, 2 * blk), 1)
    keep_first = jnp.logical_or(col >= blk, i > 0)
    scale = A_HD ** -0.5
    ones = jnp.ones((2 * blk, A_HD), BF16)
    kn = [[_rms(kp_ref[:, A_HD * g:A_HD * (g + 1)], gk)] +
          [_rms(kc_ref[blk * sb:blk * (sb + 1), A_HD * g:A_HD * (g + 1)], gk) for sb in range(nsub)]
          for g in range(A_KV_HEADS)]
    vv = [[vp_ref[:, A_HD * g:A_HD * (g + 1)]] +
          [vc_ref[blk * sb:blk * (sb + 1), A_HD * g:A_HD * (g + 1)] for sb in range(nsub)]
          for g in range(A_KV_HEADS)]
    for sb in range(nsub):
        q = q_ref[blk * sb:blk * (sb + 1), :]
        outs = [None] * A_HEADS
        for g in range(A_KV_HEADS):
            kcat = jnp.concatenate([kn[g][sb], kn[g][sb + 1]], axis=0).astype(BF16)
            vext = jnp.concatenate([jnp.concatenate([vv[g][sb], vv[g][sb + 1]], axis=0).astype(BF16), ones], axis=1)
            qg = jnp.concatenate(
                [_rms(q[:, A_HD * (A_GROUP * g + j):A_HD * (A_GROUP * g + j + 1)], gq) for j in range(A_GROUP)],
                axis=0)
            s = lax.dot_general(qg.astype(BF16), kcat, NT_DIMS, preferred_element_type=F32) * scale
            s = s + bias_ref[A_GROUP * g:A_GROUP * (g + 1)].reshape(A_GROUP * blk, 2 * blk)
            if sb == 0:
                s = jnp.where(keep_first, s, -jnp.inf)
            snk = snk_ref[g]
            m = jnp.maximum(jnp.max(s, axis=-1, keepdims=True), snk)
            p = jnp.exp(s - m)
            oext = jnp.dot(p.astype(BF16), vext, preferred_element_type=F32)
            den = oext[:, A_HD:A_HD + 1] + jnp.exp(snk - m)
            o = oext[:, :A_HD] / den
            for j in range(A_GROUP):
                outs[A_GROUP * g + j] = o[blk * j:blk * (j + 1)]
        o_ref[blk * sb:blk * (sb + 1), :] = jnp.concatenate(outs, axis=-1)
    wk_ref[0] = jnp.concatenate([kn[g][nsub] for g in range(A_KV_HEADS)], axis=-1)
    wv_ref[0] = vc_ref[blk * (nsub - 1):blk * nsub, :]


SWA_QUERY_BLOCKS = 4


def swa_prompt(proj, batch, seq, rel_bias, gq, gk, sinks):
    blk = WINDOW
    nsub = math.gcd(SWA_QUERY_BLOCKS, seq // blk)
    tq = nsub * blk
    nb = seq // tq
    qi = np.arange(blk)[:, None]
    kj = np.arange(2 * blk)[None, :]
    bucket = jnp.asarray(_t5_bucket_np(qi + blk - kj))
    snk = jnp.broadcast_to(sinks.astype(F32).reshape(A_KV_HEADS, A_GROUP, 1, 1),
                           (A_KV_HEADS, A_GROUP, blk, 1)).reshape(A_KV_HEADS, A_GROUP * blk, 1)
    cur = lambda c: (lambda b, i: (b * nb + i, c))
    prev = lambda c: (lambda b, i: (jnp.maximum((b * nb + i) * nsub - 1, b * nb * nsub), c))
    kvw = A_KV_HEADS * A_HD
    return pl.pallas_call(
        _swa_prompt_kernel,
        grid=(batch, nb),
        in_specs=[pl.BlockSpec(memory_space=pltpu.SMEM),
                  pl.BlockSpec((tq, A_HEADS * A_HD), cur(COL_AQ // (A_HEADS * A_HD))),
                  pl.BlockSpec((tq, kvw), cur(COL_AK // kvw)),
                  pl.BlockSpec((blk, kvw), prev(COL_AK // kvw)),
                  pl.BlockSpec((tq, kvw), cur(COL_AV // kvw)),
                  pl.BlockSpec((blk, kvw), prev(COL_AV // kvw)),
                  pl.BlockSpec((1, A_HD), lambda b, i: (0, 0)),
                  pl.BlockSpec((1, A_HD), lambda b, i: (0, 0)),
                  pl.BlockSpec((A_KV_HEADS, A_GROUP * blk, 1), lambda b, i: (0, 0, 0)),
                  pl.BlockSpec((blk, 2 * blk), lambda b, i: (0, 0))],
        out_specs=[pl.BlockSpec((tq, A_HEADS * A_HD), lambda b, i: (b * nb + i, 0)),
                   pl.BlockSpec((1, blk, kvw), lambda b, i: (b, 0, 0)),
                   pl.BlockSpec((1, blk, kvw), lambda b, i: (b, 0, 0))],
        out_shape=[jax.ShapeDtypeStruct((batch * seq, A_HEADS * A_HD), F32),
                   jax.ShapeDtypeStruct((batch, blk, kvw), F32),
                   jax.ShapeDtypeStruct((batch, blk, kvw), F32)],
        scratch_shapes=[pltpu.VMEM((A_HEADS, blk, 2 * blk), F32)],
        compiler_params=_params("arbitrary", "arbitrary"),
        name="swa_prompt",
    )(rel_bias.astype(F32), proj, proj, proj, proj, proj, gq.reshape(1, A_HD), gk.reshape(1, A_HD), snk, bucket)


def _swa_sample_kernel(rb_ref, snk_ref, q_ref, kn_ref, vn_ref, ck_ref, cv_ref, gq_ref, gk_ref, bkt_c_ref, bkt_n_ref,
                       o_ref, knew_ref, bias_c_ref, bias_n_ref, *, t_new):
    wc = WINDOW

    @pl.when(pl.program_id(0) == 0)
    def _():
        t = lax.broadcasted_iota(I32, (t_new, LANE), 0)
        c = lax.broadcasted_iota(I32, (t_new, LANE), 1)
        valid_c = c > t
        valid_n = c <= t
        for h in range(A_HEADS):
            bias_c_ref[h] = jnp.where(valid_c, _bias_from_buckets(bkt_c_ref[...], rb_ref, h), -jnp.inf)
            bias_n_ref[h] = jnp.where(valid_n, _bias_from_buckets(bkt_n_ref[...], rb_ref, h), -jnp.inf)

    q = q_ref[...]
    gq, gk = gq_ref[...], gk_ref[...]
    scale = A_HD ** -0.5
    kn_all = []
    kv = []
    for g in range(A_KV_HEADS):
        ks = slice(A_HD * g, A_HD * (g + 1))
        knn = _rms(kn_ref[:, :, ks], gk)
        kn_all.append(knn)
        kv.append((ck_ref[:, :, ks].astype(BF16), cv_ref[:, :, ks].astype(BF16), knn, vn_ref[:, :, ks]))
    outs = []
    for h in range(A_HEADS):
        kc, vc, knn, vnn = kv[h // A_GROUP]
        qh = _rms(q[:, :, A_HD * h:A_HD * (h + 1)], gq)
        s_c = jnp.einsum('bqd,bkd->bqk', qh.astype(BF16), kc, preferred_element_type=F32) * scale + bias_c_ref[h]
        s_n = [jnp.sum(qh * knn[:, j:j + 1, :], axis=-1, keepdims=True) * scale + bias_n_ref[h][:, j:j + 1]
               for j in range(t_new)]
        snk = snk_ref[h]
        m = jnp.maximum(jnp.max(s_c, axis=-1, keepdims=True), snk)
        for j in range(t_new):
            m = jnp.maximum(m, s_n[j])
        p_c = jnp.exp(s_c - m)
        p_n = [jnp.exp(s_n[j] - m) for j in range(t_new)]
        den = jnp.sum(p_c, axis=-1, keepdims=True) + jnp.exp(snk - m)
        for j in range(t_new):
            den = den + p_n[j]
        o = jnp.einsum('bqk,bkd->bqd', (p_c / den).astype(BF16), vc, preferred_element_type=F32)
        for j in range(t_new):
            o = o + (p_n[j] / den) * vnn[:, j:j + 1, :]
        outs.append(o)
    o_ref[...] = jnp.concatenate(outs, axis=-1)
    knew_ref[...] = jnp.concatenate(kn_all, axis=-1)


def swa_sample(proj3, cache_k, cache_v, rel_bias, gq, gk, sinks, bb):
    batch, t_new, _ = proj3.shape
    wc = cache_k.shape[1]
    kvw = A_KV_HEADS * A_HD
    t = np.arange(t_new)[:, None]
    c = np.arange(LANE)[None, :]
    bkt_c = jnp.asarray(_t5_bucket_np(wc + t - c))
    bkt_n = jnp.asarray(_t5_bucket_np(t - c))
    whole = lambda *shape: pl.BlockSpec(shape, lambda i: (0,) * len(shape))
    return pl.pallas_call(
        functools.partial(_swa_sample_kernel, t_new=t_new),
        grid=(batch // bb,),
        in_specs=[pl.BlockSpec(memory_space=pltpu.SMEM),
                  pl.BlockSpec(memory_space=pltpu.SMEM),
                  pl.BlockSpec((bb, t_new, A_HEADS * A_HD), lambda i: (i, 0, COL_AQ // (A_HEADS * A_HD))),
                  pl.BlockSpec((bb, t_new, kvw), lambda i: (i, 0, COL_AK // kvw)),
                  pl.BlockSpec((bb, t_new, kvw), lambda i: (i, 0, COL_AV // kvw)),
                  pl.BlockSpec((bb, wc, kvw), lambda i: (i, 0, 0)),
                  pl.BlockSpec((bb, wc, kvw), lambda i: (i, 0, 0)),
                  whole(1, A_HD), whole(1, A_HD), whole(t_new, LANE), whole(t_new, LANE)],
        out_specs=[pl.BlockSpec((bb, t_new, A_HEADS * A_HD), lambda i: (i, 0, 0)),
                   pl.BlockSpec((bb, t_new, kvw), lambda i: (i, 0, 0))],
        out_shape=[jax.ShapeDtypeStruct((batch, t_new, A_HEADS * A_HD), F32),
                   jax.ShapeDtypeStruct((batch, t_new, kvw), F32)],
        scratch_shapes=[pltpu.VMEM((A_HEADS, t_new, LANE), F32), pltpu.VMEM((A_HEADS, t_new, LANE), F32)],
        compiler_params=_params("arbitrary"),
        name="swa_sample",
    )(rel_bias.astype(F32), sinks.astype(F32), proj3, proj3, proj3, cache_k, cache_v,
      gq.reshape(1, A_HD), gk.reshape(1, A_HD), bkt_c, bkt_n)


def _gla_kernel(q_ref, k_ref, v_ref, lr_ref, r_ref, s0_ref, w2_ref, gb_ref, gn_ref, o_ref, sout_ref, st_ref,
                *, chunk, sub):
    c = pl.program_id(1)
    bb = q_ref.shape[0]

    @pl.when(c == 0)
    def _():
        st_ref[...] = s0_ref[...]

    gn = gn_ref[...]
    row = lax.broadcasted_iota(I32, (chunk, chunk), 0)
    col = lax.broadcasted_iota(I32, (chunk, chunk), 1)
    nsub = chunk // sub
    for i in range(bb):
        z = jnp.dot(lr_ref[i, 0].astype(BF16), w2_ref[...], preferred_element_type=F32) + gb_ref[...]
        b = (jnp.minimum(z, 0.0) - jnp.log1p(jnp.exp(-jnp.abs(z)))) / B_TAU
        row_w = lax.broadcasted_iota(I32, b.shape, 0)
        shift = 1
        while shift < chunk:
            b = b + jnp.where(row_w >= shift, pltpu.roll(b, shift, 0), 0.0)
            shift *= 2
        q = q_ref[i, 0] * (B_DK ** -0.5)
        k, v, r = k_ref[i, 0], v_ref[i, 0], r_ref[i, 0]
        outs = []
        for h in range(B_HEADS):
            ks = slice(B_DK * h, B_DK * (h + 1))
            vs = slice(B_DV * h, B_DV * (h + 1))
            qh, kh, vh, bh = q[:, ks], k[:, ks], v[:, vs], b[:, ks]
            st = st_ref[i, h]
            o = lax.dot_general((qh * jnp.exp(bh)).astype(BF16), st.astype(BF16), NT_DIMS,
                                preferred_element_type=F32)
            refs = [bh[sub * (j + 1) - 1:sub * (j + 1)] for j in range(nsub)]
            rfull = jnp.concatenate([jnp.broadcast_to(rj, (sub, B_DK)) for rj in refs], axis=0)
            kt = (kh * jnp.exp(rfull - bh)).astype(BF16)
            qcat = jnp.concatenate([qh * jnp.exp(jnp.minimum(bh - rj, 80.0)) for rj in refs], axis=0).astype(BF16)
            aa = lax.dot_general(qcat, kt, NT_DIMS, preferred_element_type=F32)
            att = jnp.zeros((chunk, chunk), F32)
            for j in range(nsub):
                in_sub = jnp.logical_and(col >= sub * j, col < sub * (j + 1))
                att = jnp.where(in_sub, aa[chunk * j:chunk * (j + 1)], att)
            att = jnp.where(col <= row, att, 0.0)
            o = o + jnp.dot(att.astype(BF16), vh.astype(BF16), preferred_element_type=F32)
            bl = bh[chunk - 1:chunk]
            kd = (kh * jnp.exp(bl - bh)).astype(BF16)
            st_ref[i, h] = st * jnp.exp(bl) + lax.dot_general(vh.astype(BF16), kd, TN_DIMS,
                                                              preferred_element_type=F32)
            rh = r[:, vs]
            outs.append(_rms(o, gn) * (rh * jax.nn.sigmoid(rh)))
        o_ref[i, 0] = jnp.concatenate(outs, axis=-1)
    sout_ref[...] = st_ref[...]


def gla(proj4, state_t, bb, w2p, gbias, gn):
    batch, nchunk, chunk, _ = proj4.shape
    sub = min(B_SUB, chunk)
    kw, vw = B_HEADS * B_DK, B_HEADS * B_DV
    blk = lambda w, col: pl.BlockSpec((bb, 1, chunk, w), lambda b, c: (b, c, 0, col // w))
    whole = lambda *shape: pl.BlockSpec(shape, lambda b, c: (0,) * len(shape))
    st_spec = pl.BlockSpec((bb, B_HEADS, B_DV, B_DK), lambda b, c: (b, 0, 0, 0))
    return pl.pallas_call(
        functools.partial(_gla_kernel, chunk=chunk, sub=sub),
        grid=(batch // bb, nchunk),
        in_specs=[blk(kw, COL_BQ), blk(kw, COL_BK), blk(vw, COL_BV), blk(LANE, COL_LR), blk(vw, COL_BR), st_spec,
                  whole(LANE, kw), whole(1, kw), whole(1, B_DV)],
        out_specs=[pl.BlockSpec((bb, 1, chunk, vw), lambda b, c: (b, c, 0, 0)), st_spec],
        out_shape=[jax.ShapeDtypeStruct((batch, nchunk, chunk, vw), F32),
                   jax.ShapeDtypeStruct((batch, B_HEADS, B_DV, B_DK), F32)],
        scratch_shapes=[pltpu.VMEM((bb, B_HEADS, B_DV, B_DK), F32)],
        compiler_params=_params("arbitrary", "arbitrary"),
        name="gla",
    )(proj4, proj4, proj4, proj4, proj4, state_t, w2p, gbias.reshape(1, kw), gn.reshape(1, B_DV))


def _xattn_kernel(q_ref, k_ref, v_ref, gq_ref, gk_ref, o_ref, *rest, bb, norm_k, token_major):
    gq, gk = gq_ref[...], gk_ref[...]
    scale = C_HD ** -0.5
    for i in range(bb):
        q = q_ref[i]
        outs, kns = [], []
        for h in range(C_HEADS):
            hs = slice(C_HD * h, C_HD * (h + 1))
            if token_major:
                rows = pl.ds(h, N_MEM, stride=C_HEADS)
                kh, vh = k_ref[i, rows, :], v_ref[i, rows, :]
            else:
                kh, vh = k_ref[i, :, hs], v_ref[i, :, hs]
            if norm_k:
                kh = _rms(kh, gk)
                kns.append(kh)
            qh = _rms(q[:, hs], gq)
            s = lax.dot_general(qh.astype(BF16), kh.astype(BF16), NT_DIMS, preferred_element_type=F32) * scale
            p = jnp.exp(s - jnp.max(s, axis=-1, keepdims=True))
            p = p / jnp.sum(p, axis=-1, keepdims=True)
            outs.append(jnp.dot(p.astype(BF16), vh.astype(BF16), preferred_element_type=F32))
        o_ref[i] = jnp.concatenate(outs, axis=-1)
        if norm_k:
            rest[0][i] = jnp.concatenate(kns, axis=-1)


def xattn(proj3, mem_k, mem_v_arr, v_col, gq, gk, bb, tq, norm_k, token_major=False):
    batch, t, _ = proj3.shape
    cw = C_HEADS * C_HD
    whole = lambda *shape: pl.BlockSpec(shape, lambda b, i: (0,) * len(shape))
    out_specs = [pl.BlockSpec((bb, tq, cw), lambda b, i: (b, i, 0))]
    out_shape = [jax.ShapeDtypeStruct((batch, t, cw), F32)]
    if norm_k:
        out_specs.append(pl.BlockSpec((bb, N_MEM, cw), lambda b, i: (b, 0, 0)))
        out_shape.append(jax.ShapeDtypeStruct((batch, N_MEM, cw), F32))
    if token_major:
        kspec = vspec = pl.BlockSpec((bb, N_MEM * C_HEADS, C_HD), lambda b, i: (b, 0, 0))
    else:
        kspec = pl.BlockSpec((bb, N_MEM, cw), lambda b, i: (b, 0, 0))
        vspec = pl.BlockSpec((bb, N_MEM, cw), lambda b, i: (b, 0, v_col))
    return pl.pallas_call(
        functools.partial(_xattn_kernel, bb=bb, norm_k=norm_k, token_major=token_major),
        grid=(batch // bb, t // tq),
        in_specs=[pl.BlockSpec((bb, tq, cw), lambda b, i: (b, i, COL_CQ // cw)), kspec, vspec,
                  whole(1, C_HD), whole(1, C_HD)],
        out_specs=out_specs,
        out_shape=out_shape,
        compiler_params=_params("arbitrary", "arbitrary"),
        name="xattn",
    )(proj3, mem_k, mem_v_arr, gq.reshape(1, C_HD), gk.reshape(1, C_HD))


def _merge_kernel(oa_ref, ob_ref, oc_ref, gl_ref, bg_ref, x_ref, wa_ref, wb_ref, wc_ref, wo_ref, g2_ref,
                  h_ref, xn_ref):
    d = D_MODEL
    gate = jax.nn.sigmoid(gl_ref[...] + bg_ref[...])
    pa = jnp.dot(oa_ref[...].astype(BF16), wa_ref[...], preferred_element_type=F32)
    pb = jnp.dot(ob_ref[...].astype(BF16), wb_ref[...], preferred_element_type=F32)
    pc = jnp.dot(oc_ref[...].astype(BF16), wc_ref[...], preferred_element_type=F32)
    merged = gate[:, :d] * pa + gate[:, d:2 * d] * pb + gate[:, 2 * d:] * pc
    h = x_ref[...] + jnp.dot(merged.astype(BF16), wo_ref[...], preferred_element_type=F32)
    h_ref[...] = h
    xn_ref[...] = _rms(h, g2_ref[...]).astype(BF16)


def merge(oa, ob, oc, proj, b_gates, x, wa, wb, wc, wo, g2, tm):
    n, d = x.shape
    bw = oa.shape[1]
    tok = lambda w: pl.BlockSpec((tm, w), lambda i: (i, 0))
    whole = lambda *shape: pl.BlockSpec(shape, lambda i: (0,) * len(shape))
    return pl.pallas_call(
        _merge_kernel,
        grid=(n // tm,),
        in_specs=[tok(bw), tok(bw), tok(bw), tok(3 * d), whole(1, 3 * d), tok(d),
                  whole(bw, d), whole(bw, d), whole(bw, d), whole(d, d), whole(1, d)],
        out_specs=[tok(d), tok(d)],
        out_shape=[jax.ShapeDtypeStruct((n, d), F32), jax.ShapeDtypeStruct((n, d), BF16)],
        compiler_params=_params("arbitrary"),
        name="merge",
    )(oa, ob, oc, proj, b_gates.reshape(1, 3 * d), x, wa, wb, wc, wo, g2.reshape(1, d))


_UNRANKED = float(P_NKEYS)


def _top16_exact(s):
    nrow = s.shape[0]
    row = lax.broadcasted_iota(I32, s.shape, 0)
    rank = jnp.full(s.shape, _UNRANKED, F32)
    vals = []
    for kk in range(P_TOPK):
        m = jnp.max(s, axis=0, keepdims=True)
        idx = jnp.min(jnp.where(s == m, row, nrow), axis=0, keepdims=True)
        hit = row == idx
        rank = jnp.where(hit, float(kk), rank)
        s = jnp.where(hit, -jnp.inf, s)
        vals.append(m)
    return jnp.concatenate(vals, axis=0), rank


def _top16_distinct(s, want_rank=True):
    rank = jnp.zeros(s.shape, F32)
    m = jnp.max(s, axis=0, keepdims=True)
    vals = [m]
    for kk in range(1, P_TOPK):
        below = s < m
        if want_rank:
            rank = jnp.where(below, float(kk), rank)
        m = jnp.max(jnp.where(below, s, -jnp.inf), axis=0, keepdims=True)
        vals.append(m)
    below = s < m
    rank = jnp.where(below, _UNRANKED, rank)
    count = jnp.sum(jnp.where(below, 0.0, 1.0), axis=0, keepdims=True)
    return jnp.concatenate(vals, axis=0), rank, below, count == float(P_TOPK)


_CAND_ROWS = 16 + 7 * 8 + 8


def _candidates(v1, v2):
    blocks = [v1[0:1] + v2[0:16]]
    blocks += [v1[a:a + 1] + v2[0:8] for a in range(1, 8)]
    blocks.append(v1[8:16] + v2[0:1])
    return jnp.concatenate(blocks, axis=0)


def _route_store(refs, ls, s1, s2, v1, rank1, v2, rank2, cand, sel):
    nb_ref, e1_ref, r2_ref, e2_ref = refs
    cmax = v1[0:1] + v2[0:1]
    zsum = jnp.sum(jnp.where(sel, jnp.exp(cand - cmax), 0.0), axis=0, keepdims=True)
    self32 = jnp.where(sel, 1.0, 0.0)
    nb = jnp.zeros(s1.shape, F32)
    nb = jnp.where(rank1 == 0.0, jnp.sum(self32[0:16], axis=0, keepdims=True), nb)
    for a in range(1, 8):
        cnt = jnp.sum(self32[16 + 8 * (a - 1):16 + 8 * a], axis=0, keepdims=True)
        nb = jnp.where(rank1 == float(a), cnt, nb)
    for a in range(8, 16):
        nb = jnp.where(rank1 == float(a), self32[72 + a - 8:73 + a - 8], nb)
    nb_ref[0, :, ls] = nb
    e1_ref[0, :, ls] = jnp.where(rank1 < float(P_TOPK), jnp.exp(s1 - v1[0:1]), 0.0)
    e2_ref[0, :, ls] = jnp.where(rank2 < float(P_TOPK), jnp.exp(s2 - v2[0:1]) / zsum, 0.0)
    r2_ref[0, :, ls] = pltpu.bitcast(rank2.astype(BF16), jnp.uint32)


def _peer_route_kernel(xn_ref, wq_ref, k1_ref, k2_ref, nb_ref, e1_ref, r2_ref, e2_ref, s_ref):
    tb = xn_ref.shape[0]
    refs = (nb_ref, e1_ref, r2_ref, e2_ref)
    qt = lax.dot_general(wq_ref[...], xn_ref[...], NT_DIMS, preferred_element_type=F32)
    half = qt.shape[0] // 2
    s_ref[0] = jnp.dot(k1_ref[...], qt[:half].astype(BF16), preferred_element_type=F32)
    s_ref[1] = jnp.dot(k2_ref[...], qt[half:].astype(BF16), preferred_element_type=F32)
    strips = [slice(LANE * st, LANE * (st + 1)) for st in range(tb // LANE)]
    bad = jnp.zeros((1, LANE), F32)
    for ls in strips:
        s1 = s_ref[0, :, ls]
        s2 = s_ref[1, :, ls]
        v1, rank1, _, ok1 = _top16_distinct(s1)
        v2, rank2, _, ok2 = _top16_distinct(s2)
        cand = _candidates(v1, v2)
        _, _, cbelow, okc = _top16_distinct(cand, want_rank=False)
        _route_store(refs, ls, s1, s2, v1, rank1, v2, rank2, cand, jnp.logical_not(cbelow))
        ok = jnp.logical_and(jnp.logical_and(ok1, ok2), okc)
        bad = jnp.maximum(bad, jnp.where(ok, 0.0, 1.0))

    @pl.when(jnp.max(bad) > 0.5)
    def _():
        for ls in strips:
            s1 = s_ref[0, :, ls]
            s2 = s_ref[1, :, ls]
            v1, rank1 = _top16_exact(s1)
            v2, rank2 = _top16_exact(s2)
            cand = _candidates(v1, v2)
            _, crank = _top16_exact(cand)
            _route_store(refs, ls, s1, s2, v1, rank1, v2, rank2, cand, crank < float(P_TOPK))


def peer_route(xn, wq_t, k1, k2, tb):
    n, d = xn.shape
    hw = wq_t.shape[0] // P_HEADS
    ospec = pl.BlockSpec((1, P_NKEYS, tb), lambda i, h: (h, 0, i))
    pspec = pl.BlockSpec((1, P_NKEYS // 2, tb), lambda i, h: (h, 0, i))
    shape = lambda dt: jax.ShapeDtypeStruct((P_HEADS, P_NKEYS, n), dt)
    packed = jax.ShapeDtypeStruct((P_HEADS, P_NKEYS // 2, n), jnp.uint32)
    return pl.pallas_call(
        _peer_route_kernel,
        grid=(n // tb, P_HEADS),
        in_specs=[pl.BlockSpec((tb, d), lambda i, h: (i, 0)),
                  pl.BlockSpec((hw, d), lambda i, h: (h, 0)),
                  pl.BlockSpec((P_NKEYS, hw // 2), lambda i, h: (0, 0)),
                  pl.BlockSpec((P_NKEYS, hw // 2), lambda i, h: (0, 0))],
        out_specs=[ospec, ospec, pspec, ospec],
        out_shape=[shape(F32), shape(F32), packed, shape(F32)],
        scratch_shapes=[pltpu.VMEM((2, P_NKEYS, tb), F32)],
        compiler_params=_params("arbitrary", "arbitrary"),
        name="peer_route",
    )(xn, wq_t, k1, k2)


MXU_DIM = 256


BF16_ROWS = 16
C_GROUP = 2


def _peer_dense_kernel(xn_ref, u_ref, vt_ref, nb_ref, e1_ref, r2_ref, e2_ref, h_ref, y_ref,
                       acc_ref, hid_ref, act_ref):
    j = pl.program_id(1)
    tb = xn_ref.shape[0]
    eb = u_ref.shape[0]
    n1 = eb // P_NKEYS
    nrg = P_NKEYS // BF16_ROWS
    half = BF16_ROWS // 2

    @pl.when(j == 0)
    def _():
        acc_ref[...] = jnp.zeros_like(acc_ref)

    hid_ref[...] = lax.dot_general(u_ref[...], xn_ref[...], NT_DIMS, preferred_element_type=F32)
    zero = jnp.zeros((), BF16)

    def row_vreg(ref, c, h, ls):
        row = ref[pl.ds(h * P_NKEYS + j * n1 + c, 1), :]
        return jnp.broadcast_to(row[:, ls], (BF16_ROWS, LANE)).astype(BF16)

    for st in range(tb // LANE):
        ls = slice(LANE * st, LANE * (st + 1))
        for c0 in range(0, n1, C_GROUP):
            nbv = [[row_vreg(nb_ref, c0 + k, h, ls) for h in range(P_HEADS)] for k in range(C_GROUP)]
            e1v = [[row_vreg(e1_ref, c0 + k, h, ls) for h in range(P_HEADS)] for k in range(C_GROUP)]
            for rg in range(nrg):
                ws = slice(half * rg, half * (rg + 1))
                gs = [jnp.zeros((BF16_ROWS, LANE), BF16) for _ in range(C_GROUP)]
                for h in range(P_HEADS):
                    r2 = pltpu.bitcast(r2_ref[h, ws, ls], BF16)
                    e2 = e2_ref[h, BF16_ROWS * rg:BF16_ROWS * (rg + 1), ls].astype(BF16)
                    for k in range(C_GROUP):
                        gs[k] = gs[k] + jnp.where(r2 < nbv[k][h], e2, zero) * e1v[k][h]
                for k in range(C_GROUP):
                    rs = slice(P_NKEYS * (c0 + k) + BF16_ROWS * rg, P_NKEYS * (c0 + k) + BF16_ROWS * (rg + 1))
                    hh = hid_ref[rs, ls]
                    gelu = 0.5 * hh * (1.0 + lax.erf(hh * math.sqrt(0.5)))
                    act_ref[rs, ls] = gelu.astype(BF16) * gs[k]
    acc_ref[...] += jnp.dot(vt_ref[...], act_ref[...], preferred_element_type=F32)

    @pl.when(j == pl.num_programs(1) - 1)
    def _():
        y_ref[...] = h_ref[...] + acc_ref[...].T


def peer_dense(xn, u, vt, nb, e1, r2, e2, h, tb, eb):
    n, d = xn.shape
    ne = u.shape[0]
    rspec = pl.BlockSpec((P_HEADS * P_NKEYS, tb), lambda i, j: (0, i))
    pspec = pl.BlockSpec((P_HEADS, P_NKEYS // 2, tb), lambda i, j: (0, 0, i))
    nb = nb.reshape(P_HEADS * P_NKEYS, n)
    e1 = e1.reshape(P_HEADS * P_NKEYS, n)
    return pl.pallas_call(
        _peer_dense_kernel,
        grid=(n // tb, ne // eb),
        in_specs=[pl.BlockSpec((tb, d), lambda i, j: (i, 0)),
                  pl.BlockSpec((eb, d), lambda i, j: (j, 0)),
                  pl.BlockSpec((d, eb), lambda i, j: (0, j)),
                  rspec, rspec, pspec, pl.BlockSpec((P_HEADS, P_NKEYS, tb), lambda i, j: (0, 0, i)),
                  pl.BlockSpec((tb, d), lambda i, j: (i, 0))],
        out_specs=pl.BlockSpec((tb, d), lambda i, j: (i, 0)),
        out_shape=jax.ShapeDtypeStruct((n, d), F32),
        scratch_shapes=[pltpu.VMEM((d, tb), F32), pltpu.VMEM((eb, tb), F32), pltpu.VMEM((eb, tb), BF16)],
        compiler_params=_params("arbitrary", "arbitrary"),
        name="peer_dense",
    )(xn, u, vt, nb, e1, r2, e2, h)


TOKEN_TILE = 512
PEER_EXPERT_TILE = 8 * P_NKEYS
SAMPLE_BATCH_TILE = 8


def _permute_w_in(w_in):
    d = w_in.shape[0]
    seg = lambda a, b: w_in[:, a:b]
    lr = jnp.pad(seg(1792, 1808), ((0, 0), (0, LANE - B_RANK)))
    cols = [seg(2832, 5904), seg(0, 512), seg(1280, 1792), seg(1808, 2320), seg(2320, 2832), seg(768, 1024),
            seg(1024, 1280), seg(512, 640), seg(640, 768), lr, jnp.zeros((d, IN_COLS - COL_LR - LANE), w_in.dtype)]
    return jnp.concatenate(cols, axis=1).astype(BF16)


def kernel(x_prompt, x_sample, mem_prompt, cache_win_k, cache_win_v, state_gla, cache_mem_k, cache_mem_v, norm1_g, w_in, b_gates, a_qnorm_g, a_knorm_g, a_sinks, rel_bias, b_gate_w2, b_gate_b, b_gn_g, c_qnorm_g, c_knorm_g, mem_norm_g, w_mem_kv, w_br_a, w_br_b, w_br_c, w_out, norm2_g, peer_wq, peer_k1, peer_k2, peer_u, peer_v):
    depth = w_in.shape[0]
    assert depth == 1, "one layer"
    l = 0
    bp, tp, d = x_prompt.shape
    bs, ts, _ = x_sample.shape
    cw = C_HEADS * C_HD
    kvw = A_KV_HEADS * A_HD

    w_all = _permute_w_in(w_in[l])
    w2p = jnp.pad(b_gate_w2[l], ((0, LANE - B_RANK), (0, 0))).astype(BF16)
    wa, wb, wc, wo = (w.astype(BF16) for w in (w_br_a[l], w_br_b[l], w_br_c[l], w_out[l]))
    wq_t = peer_wq[l].T.astype(BF16)
    k1, k2 = peer_k1[l].astype(BF16), peer_k2[l].astype(BF16)
    u = peer_u[l].astype(BF16)
    vt = peer_v[l].T.astype(BF16)

    def channel_mix(oa, ob, oc, proj, x2):
        tm = min(TOKEN_TILE, x2.shape[0])
        h, xn2 = merge(oa, ob, oc, proj, b_gates[l], x2, wa, wb, wc, wo, norm2_g[l], tm)
        nb, e1, r2, e2 = peer_route(xn2, wq_t, k1, k2, tm)
        return peer_dense(xn2, u, vt, nb, e1, r2, e2, h, tm, PEER_EXPERT_TILE)

    xp = x_prompt.reshape(bp * tp, d)
    proj_p = norm_matmul(xp, norm1_g[l], w_all, min(TOKEN_TILE, bp * tp), IN_COLS // 2)
    memkv = norm_matmul(mem_prompt.reshape(bp * N_MEM, d), mem_norm_g[l], w_mem_kv[l].astype(BF16), min(TOKEN_TILE, bp * N_MEM), 2 * cw)
    oa_p, wk_p, wv_p = swa_prompt(proj_p, bp, tp, rel_bias, a_qnorm_g[l], a_knorm_g[l], a_sinks[l])
    nchunk = tp // B_CHUNK
    ob_p, st_p = gla(proj_p.reshape(bp, nchunk, B_CHUNK, IN_COLS), jnp.zeros((bp, B_HEADS, B_DV, B_DK), F32),
                     bp, w2p, b_gate_b[l], b_gn_g[l])
    memkv3 = memkv.reshape(bp, N_MEM, 2 * cw)
    oc_p, mk_p = xattn(proj_p.reshape(bp, tp, IN_COLS), memkv3, memkv3, 1, c_qnorm_g[l], c_knorm_g[l],
                       1, min(TOKEN_TILE, tp), True)
    y_p = channel_mix(oa_p, ob_p.reshape(bp * tp, -1), oc_p.reshape(bp * tp, cw), proj_p, xp)

    xs = x_sample.reshape(bs * ts, d)
    proj_s = norm_matmul(xs, norm1_g[l], w_all, min(TOKEN_TILE, bs * ts), IN_COLS // 2)
    proj_s3 = proj_s.reshape(bs, ts, IN_COLS)
    ck = cache_win_k[l].reshape(bs, -1, kvw)
    cv = cache_win_v[l].reshape(bs, -1, kvw)
    oa_s, knew = swa_sample(proj_s3, ck, cv, rel_bias, a_qnorm_g[l], a_knorm_g[l], a_sinks[l], SAMPLE_BATCH_TILE)
    vnew = proj_s3[:, :, COL_AV:COL_AV + kvw]
    ob_s, st_s = gla(proj_s3.reshape(bs, 1, ts, IN_COLS), jnp.swapaxes(state_gla[l], -1, -2),
                     math.gcd(bs, SAMPLE_BATCH_TILE), w2p, b_gate_b[l], b_gn_g[l])
    oc_s, = xattn(proj_s3, cache_mem_k[l].reshape(bs, N_MEM * C_HEADS, C_HD),
                  cache_mem_v[l].reshape(bs, N_MEM * C_HEADS, C_HD), 0,
                  c_qnorm_g[l], c_knorm_g[l], SAMPLE_BATCH_TILE, ts, False, token_major=True)
    y_s = channel_mix(oa_s.reshape(bs * ts, -1), ob_s.reshape(bs * ts, -1), oc_s.reshape(bs * ts, cw), proj_s, xs)

    wc_len = ck.shape[1]
    new_wk_s = jnp.concatenate([ck, knew], axis=1)[:, -wc_len:]
    new_wv_s = jnp.concatenate([cv, vnew], axis=1)[:, -wc_len:]
    return (y_p.reshape(bp, tp, d), y_s.reshape(bs, ts, d),
            wk_p.reshape(1, bp, WINDOW, A_KV_HEADS, A_HD), wv_p.reshape(1, bp, WINDOW, A_KV_HEADS, A_HD),
            jnp.swapaxes(st_p, -1, -2)[None],
            mk_p.reshape(1, bp, N_MEM, C_HEADS, C_HD), memkv3[:, :, cw:].reshape(1, bp, N_MEM, C_HEADS, C_HD),
            new_wk_s.reshape(1, bs, wc_len, A_KV_HEADS, A_HD), new_wv_s.reshape(1, bs, wc_len, A_KV_HEADS, A_HD),
            jnp.swapaxes(st_s, -1, -2)[None])
```

```python
import functools
import math

import numpy as np
import jax
import jax.numpy as jnp
from jax import lax
from jax.experimental import pallas as pl
from jax.experimental.pallas import tpu as pltpu

F32 = jnp.float32
BF16 = jnp.bfloat16
I32 = jnp.int32

EPS = 1e-6
D_MODEL = 1024
N_MEM = 256
A_HEADS, A_KV_HEADS, A_GROUP, A_HD = 8, 2, 4, 64
WINDOW = 128
N_BUCKETS, MAX_DISTANCE = 32, 128
B_HEADS, B_DK, B_DV, B_RANK, B_TAU = 4, 64, 128, 16, 16.0
B_CHUNK, B_SUB = 64, 16
C_HEADS, C_HD = 4, 128
P_HEADS, P_NKEYS, P_TOPK = 8, 128, 16
P_EXPERTS = P_NKEYS * P_NKEYS

LANE = 128
COL_GL, COL_AQ, COL_BV, COL_BR, COL_CQ, COL_BQ, COL_BK, COL_AK, COL_AV, COL_LR = (
    0, 3072, 3584, 4096, 4608, 5120, 5376, 5632, 5760, 5888)
IN_COLS = 6144
VMEM_LIMIT = 56 * 1024 * 1024

NT_DIMS = (((1,), (1,)), ((), ()))
TN_DIMS = (((0,), (0,)), ((), ()))


def _rms(x, g):
    ms = jnp.mean(x * x, axis=-1, keepdims=True)
    return x * lax.rsqrt(ms + EPS) * g


def _params(*sem, flags=None):
    return pltpu.CompilerParams(dimension_semantics=sem, vmem_limit_bytes=VMEM_LIMIT, flags=flags)


def _norm_matmul_kernel(x_ref, g_ref, w_ref, o_ref):
    xn = _rms(x_ref[...], g_ref[...]).astype(BF16)
    o_ref[...] = jnp.dot(xn, w_ref[...], preferred_element_type=F32)


def norm_matmul(x, g, w, tm, tn):
    n, d = x.shape
    m = w.shape[1]
    return pl.pallas_call(
        _norm_matmul_kernel,
        grid=(m // tn, n // tm),
        in_specs=[pl.BlockSpec((tm, d), lambda j, i: (i, 0)),
                  pl.BlockSpec((1, d), lambda j, i: (0, 0)),
                  pl.BlockSpec((d, tn), lambda j, i: (0, j))],
        out_specs=pl.BlockSpec((tm, tn), lambda j, i: (i, j)),
        out_shape=jax.ShapeDtypeStruct((n, m), F32),
        compiler_params=_params("arbitrary", "arbitrary"),
        name="norm_matmul",
    )(x, g.reshape(1, d), w)


def _t5_bucket_np(dist):
    n = np.maximum(dist, 0)
    max_exact = N_BUCKETS // 2
    nf = np.maximum(n, 1).astype(np.float32)
    large = max_exact + (np.log(nf / np.float32(max_exact)) / np.float32(math.log(MAX_DISTANCE / max_exact))
                         * np.float32(N_BUCKETS - max_exact)).astype(np.int32)
    large = np.minimum(large, N_BUCKETS - 1)
    return np.where(n < max_exact, n, large).astype(np.int32)


def _bias_from_buckets(bucket, rb_ref, head):
    tab = jnp.zeros(bucket.shape, F32)
    for bk in range(N_BUCKETS):
        tab = jnp.where(bucket == bk, rb_ref[bk, head], tab)
    return tab


def _swa_prompt_kernel(rb_ref, q_ref, kc_ref, kp_ref, vc_ref, vp_ref, gq_ref, gk_ref, snk_ref, bucket_ref,
                       o_ref, wk_ref, wv_ref, bias_ref):
    blk = WINDOW
    first = jnp.logical_and(pl.program_id(0) == 0, pl.program_id(1) == 0)

    @pl.when(first)
    def _():
        bucket = bucket_ref[...]
        qi = lax.broadcasted_iota(I32, (blk, 2 * blk), 0)
        kj = lax.broadcasted_iota(I32, (blk, 2 * blk), 1)
        dist = qi + blk - kj
        band = jnp.logical_and(dist >= 0, dist < WINDOW)
        for h in range(A_HEADS):
            bias_ref[h] = jnp.where(band, _bias_from_buckets(bucket, rb_ref, h), -jnp.inf)

    i = pl.program_id(1)
    nsub = q_ref.shape[0] // blk
    gq, gk = gq_ref[...], gk_ref[...]
    col = lax.broadcasted_iota(I32, (A_GROUP * blk, 2 * blk), 1)
    keep_first = jnp.logical_or(col >= blk, i > 0)
    scale = A_HD ** -0.5
    ones = jnp.ones((2 * blk, A_HD), BF16)
    kn = [[_rms(kp_ref[:, A_HD * g:A_HD * (g + 1)], gk)] +
          [_rms(kc_ref[blk * sb:blk * (sb + 1), A_HD * g:A_HD * (g + 1)], gk) for sb in range(nsub)]
          for g in range(A_KV_HEADS)]
    vv = [[vp_ref[:, A_HD * g:A_HD * (g + 1)]] +
          [vc_ref[blk * sb:blk * (sb + 1), A_HD * g:A_HD * (g + 1)] for sb in range(nsub)]
          for g in range(A_KV_HEADS)]
    for sb in range(nsub):
        q = q_ref[blk * sb:blk * (sb + 1), :]
        outs = [None] * A_HEADS
        for g in range(A_KV_HEADS):
            kcat = jnp.concatenate([kn[g][sb], kn[g][sb + 1]], axis=0).astype(BF16)
            vext = jnp.concatenate([jnp.concatenate([vv[g][sb], vv[g][sb + 1]], axis=0).astype(BF16), ones], axis=1)
            qg = jnp.concatenate(
                [_rms(q[:, A_HD * (A_GROUP * g + j):A_HD * (A_GROUP * g + j + 1)], gq) for j in range(A_GROUP)],
                axis=0)
            s = lax.dot_general(qg.astype(BF16), kcat, NT_DIMS, preferred_element_type=F32) * scale
            s = s + bias_ref[A_GROUP * g:A_GROUP * (g + 1)].reshape(A_GROUP * blk, 2 * blk)
            if sb == 0:
                s = jnp.where(keep_first, s, -jnp.inf)
            snk = snk_ref[g]
            m = jnp.maximum(jnp.max(s, axis=-1, keepdims=True), snk)
            p = jnp.exp(s - m)
            oext = jnp.dot(p.astype(BF16), vext, preferred_element_type=F32)
            den = oext[:, A_HD:A_HD + 1] + jnp.exp(snk - m)
            o = oext[:, :A_HD] / den
            for j in range(A_GROUP):
                outs[A_GROUP * g + j] = o[blk * j:blk * (j + 1)]
        o_ref[blk * sb:blk * (sb + 1), :] = jnp.concatenate(outs, axis=-1)
    wk_ref[0] = jnp.concatenate([kn[g][nsub] for g in range(A_KV_HEADS)], axis=-1)
    wv_ref[0] = vc_ref[blk * (nsub - 1):blk * nsub, :]


SWA_QUERY_BLOCKS = 4


def swa_prompt(proj, batch, seq, rel_bias, gq, gk, sinks):
    blk = WINDOW
    nsub = math.gcd(SWA_QUERY_BLOCKS, seq // blk)
    tq = nsub * blk
    nb = seq // tq
    qi = np.arange(blk)[:, None]
    kj = np.arange(2 * blk)[None, :]
    bucket = jnp.asarray(_t5_bucket_np(qi + blk - kj))
    snk = jnp.broadcast_to(sinks.astype(F32).reshape(A_KV_HEADS, A_GROUP, 1, 1),
                           (A_KV_HEADS, A_GROUP, blk, 1)).reshape(A_KV_HEADS, A_GROUP * blk, 1)
    cur = lambda c: (lambda b, i: (b * nb + i, c))
    prev = lambda c: (lambda b, i: (jnp.maximum((b * nb + i) * nsub - 1, b * nb * nsub), c))
    kvw = A_KV_HEADS * A_HD
    return pl.pallas_call(
        _swa_prompt_kernel,
        grid=(batch, nb),
        in_specs=[pl.BlockSpec(memory_space=pltpu.SMEM),
                  pl.BlockSpec((tq, A_HEADS * A_HD), cur(COL_AQ // (A_HEADS * A_HD))),
                  pl.BlockSpec((tq, kvw), cur(COL_AK // kvw)),
                  pl.BlockSpec((blk, kvw), prev(COL_AK // kvw)),
                  pl.BlockSpec((tq, kvw), cur(COL_AV // kvw)),
                  pl.BlockSpec((blk, kvw), prev(COL_AV // kvw)),
                  pl.BlockSpec((1, A_HD), lambda b, i: (0, 0)),
                  pl.BlockSpec((1, A_HD), lambda b, i: (0, 0)),
                  pl.BlockSpec((A_KV_HEADS, A_GROUP * blk, 1), lambda b, i: (0, 0, 0)),
                  pl.BlockSpec((blk, 2 * blk), lambda b, i: (0, 0))],
        out_specs=[pl.BlockSpec((tq, A_HEADS * A_HD), lambda b, i: (b * nb + i, 0)),
                   pl.BlockSpec((1, blk, kvw), lambda b, i: (b, 0, 0)),
                   pl.BlockSpec((1, blk, kvw), lambda b, i: (b, 0, 0))],
        out_shape=[jax.ShapeDtypeStruct((batch * seq, A_HEADS * A_HD), F32),
                   jax.ShapeDtypeStruct((batch, blk, kvw), F32),
                   jax.ShapeDtypeStruct((batch, blk, kvw), F32)],
        scratch_shapes=[pltpu.VMEM((A_HEADS, blk, 2 * blk), F32)],
        compiler_params=_params("arbitrary", "arbitrary"),
        name="swa_prompt",
    )(rel_bias.astype(F32), proj, proj, proj, proj, proj, gq.reshape(1, A_HD), gk.reshape(1, A_HD), snk, bucket)


def _swa_sample_kernel(rb_ref, snk_ref, q_ref, kn_ref, vn_ref, ck_ref, cv_ref, gq_ref, gk_ref, bkt_c_ref, bkt_n_ref,
                       o_ref, knew_ref, bias_c_ref, bias_n_ref, *, t_new):
    wc = WINDOW

    @pl.when(pl.program_id(0) == 0)
    def _():
        t = lax.broadcasted_iota(I32, (t_new, LANE), 0)
        c = lax.broadcasted_iota(I32, (t_new, LANE), 1)
        valid_c = c > t
        valid_n = c <= t
        for h in range(A_HEADS):
            bias_c_ref[h] = jnp.where(valid_c, _bias_from_buckets(bkt_c_ref[...], rb_ref, h), -jnp.inf)
            bias_n_ref[h] = jnp.where(valid_n, _bias_from_buckets(bkt_n_ref[...], rb_ref, h), -jnp.inf)

    q = q_ref[...]
    gq, gk = gq_ref[...], gk_ref[...]
    scale = A_HD ** -0.5
    kn_all = []
    kv = []
    for g in range(A_KV_HEADS):
        ks = slice(A_HD * g, A_HD * (g + 1))
        knn = _rms(kn_ref[:, :, ks], gk)
        kn_all.append(knn)
        kv.append((ck_ref[:, :, ks].astype(BF16), cv_ref[:, :, ks].astype(BF16), knn, vn_ref[:, :, ks]))
    outs = []
    for h in range(A_HEADS):
        kc, vc, knn, vnn = kv[h // A_GROUP]
        qh = _rms(q[:, :, A_HD * h:A_HD * (h + 1)], gq)
        s_c = jnp.einsum('bqd,bkd->bqk', qh.astype(BF16), kc, preferred_element_type=F32) * scale + bias_c_ref[h]
        s_n = [jnp.sum(qh * knn[:, j:j + 1, :], axis=-1, keepdims=True) * scale + bias_n_ref[h][:, j:j + 1]
               for j in range(t_new)]
        snk = snk_ref[h]
        m = jnp.maximum(jnp.max(s_c, axis=-1, keepdims=True), snk)
        for j in range(t_new):
            m = jnp.maximum(m, s_n[j])
        p_c = jnp.exp(s_c - m)
        p_n = [jnp.exp(s_n[j] - m) for j in range(t_new)]
        den = jnp.sum(p_c, axis=-1, keepdims=True) + jnp.exp(snk - m)
        for j in range(t_new):
            den = den + p_n[j]
        o = jnp.einsum('bqk,bkd->bqd', (p_c / den).astype(BF16), vc, preferred_element_type=F32)
        for j in range(t_new):
            o = o + (p_n[j] / den) * vnn[:, j:j + 1, :]
        outs.append(o)
    o_ref[...] = jnp.concatenate(outs, axis=-1)
    knew_ref[...] = jnp.concatenate(kn_all, axis=-1)


def swa_sample(proj3, cache_k, cache_v, rel_bias, gq, gk, sinks, bb):
    batch, t_new, _ = proj3.shape
    wc = cache_k.shape[1]
    kvw = A_KV_HEADS * A_HD
    t = np.arange(t_new)[:, None]
    c = np.arange(LANE)[None, :]
    bkt_c = jnp.asarray(_t5_bucket_np(wc + t - c))
    bkt_n = jnp.asarray(_t5_bucket_np(t - c))
    whole = lambda *shape: pl.BlockSpec(shape, lambda i: (0,) * len(shape))
    return pl.pallas_call(
        functools.partial(_swa_sample_kernel, t_new=t_new),
        grid=(batch // bb,),
        in_specs=[pl.BlockSpec(memory_space=pltpu.SMEM),
                  pl.BlockSpec(memory_space=pltpu.SMEM),
                  pl.BlockSpec((bb, t_new, A_HEADS * A_HD), lambda i: (i, 0, COL_AQ // (A_HEADS * A_HD))),
                  pl.BlockSpec((bb, t_new, kvw), lambda i: (i, 0, COL_AK // kvw)),
                  pl.BlockSpec((bb, t_new, kvw), lambda i: (i, 0, COL_AV // kvw)),
                  pl.BlockSpec((bb, wc, kvw), lambda i: (i, 0, 0)),
                  pl.BlockSpec((bb, wc, kvw), lambda i: (i, 0, 0)),
                  whole(1, A_HD), whole(1, A_HD), whole(t_new, LANE), whole(t_new, LANE)],
        out_specs=[pl.BlockSpec((bb, t_new, A_HEADS * A_HD), lambda i: (i, 0, 0)),
                   pl.BlockSpec((bb, t_new, kvw), lambda i: (i, 0, 0))],
        out_shape=[jax.ShapeDtypeStruct((batch, t_new, A_HEADS * A_HD), F32),
                   jax.ShapeDtypeStruct((batch, t_new, kvw), F32)],
        scratch_shapes=[pltpu.VMEM((A_HEADS, t_new, LANE), F32), pltpu.VMEM((A_HEADS, t_new, LANE), F32)],
        compiler_params=_params("arbitrary"),
        name="swa_sample",
    )(rel_bias.astype(F32), sinks.astype(F32), proj3, proj3, proj3, cache_k, cache_v,
      gq.reshape(1, A_HD), gk.reshape(1, A_HD), bkt_c, bkt_n)


def _gla_kernel(q_ref, k_ref, v_ref, lr_ref, r_ref, s0_ref, w2_ref, gb_ref, gn_ref, o_ref, sout_ref, st_ref,
                *, chunk, sub):
    c = pl.program_id(1)
    bb = q_ref.shape[0]

    @pl.when(c == 0)
    def _():
        st_ref[...] = s0_ref[...]

    gn = gn_ref[...]
    row = lax.broadcasted_iota(I32, (chunk, chunk), 0)
    col = lax.broadcasted_iota(I32, (chunk, chunk), 1)
    nsub = chunk // sub
    for i in range(bb):
        z = jnp.dot(lr_ref[i, 0].astype(BF16), w2_ref[...], preferred_element_type=F32) + gb_ref[...]
        b = (jnp.minimum(z, 0.0) - jnp.log1p(jnp.exp(-jnp.abs(z)))) / B_TAU
        row_w = lax.broadcasted_iota(I32, b.shape, 0)
        shift = 1
        while shift < chunk:
            b = b + jnp.where(row_w >= shift, pltpu.roll(b, shift, 0), 0.0)
            shift *= 2
        q = q_ref[i, 0] * (B_DK ** -0.5)
        k, v, r = k_ref[i, 0], v_ref[i, 0], r_ref[i, 0]
        outs = []
        for h in range(B_HEADS):
            ks = slice(B_DK * h, B_DK * (h + 1))
            vs = slice(B_DV * h, B_DV * (h + 1))
            qh, kh, vh, bh = q[:, ks], k[:, ks], v[:, vs], b[:, ks]
            st = st_ref[i, h]
            o = lax.dot_general((qh * jnp.exp(bh)).astype(BF16), st.astype(BF16), NT_DIMS,
                                preferred_element_type=F32)
            refs = [bh[sub * (j + 1) - 1:sub * (j + 1)] for j in range(nsub)]
            rfull = jnp.concatenate([jnp.broadcast_to(rj, (sub, B_DK)) for rj in refs], axis=0)
            kt = (kh * jnp.exp(rfull - bh)).astype(BF16)
            qcat = jnp.concatenate([qh * jnp.exp(jnp.minimum(bh - rj, 80.0)) for rj in refs], axis=0).astype(BF16)
            aa = lax.dot_general(qcat, kt, NT_DIMS, preferred_element_type=F32)
            att = jnp.zeros((chunk, chunk), F32)
            for j in range(nsub):
                in_sub = jnp.logical_and(col >= sub * j, col < sub * (j + 1))
                att = jnp.where(in_sub, aa[chunk * j:chunk * (j + 1)], att)
            att = jnp.where(col <= row, att, 0.0)
            o = o + jnp.dot(att.astype(BF16), vh.astype(BF16), preferred_element_type=F32)
            bl = bh[chunk - 1:chunk]
            kd = (kh * jnp.exp(bl - bh)).astype(BF16)
            st_ref[i, h] = st * jnp.exp(bl) + lax.dot_general(vh.astype(BF16), kd, TN_DIMS,
                                                              preferred_element_type=F32)
            rh = r[:, vs]
            outs.append(_rms(o, gn) * (rh * jax.nn.sigmoid(rh)))
        o_ref[i, 0] = jnp.concatenate(outs, axis=-1)
    sout_ref[...] = st_ref[...]


def gla(proj4, state_t, bb, w2p, gbias, gn):
    batch, nchunk, chunk, _ = proj4.shape
    sub = min(B_SUB, chunk)
    kw, vw = B_HEADS * B_DK, B_HEADS * B_DV
    blk = lambda w, col: pl.BlockSpec((bb, 1, chunk, w), lambda b, c: (b, c, 0, col // w))
    whole = lambda *shape: pl.BlockSpec(shape, lambda b, c: (0,) * len(shape))
    st_spec = pl.BlockSpec((bb, B_HEADS, B_DV, B_DK), lambda b, c: (b, 0, 0, 0))
    return pl.pallas_call(
        functools.partial(_gla_kernel, chunk=chunk, sub=sub),
        grid=(batch // bb, nchunk),
        in_specs=[blk(kw, COL_BQ), blk(kw, COL_BK), blk(vw, COL_BV), blk(LANE, COL_LR), blk(vw, COL_BR), st_spec,
                  whole(LANE, kw), whole(1, kw), whole(1, B_DV)],
        out_specs=[pl.BlockSpec((bb, 1, chunk, vw), lambda b, c: (b, c, 0, 0)), st_spec],
        out_shape=[jax.ShapeDtypeStruct((batch, nchunk, chunk, vw), F32),
                   jax.ShapeDtypeStruct((batch, B_HEADS, B_DV, B_DK), F32)],
        scratch_shapes=[pltpu.VMEM((bb, B_HEADS, B_DV, B_DK), F32)],
        compiler_params=_params("arbitrary", "arbitrary"),
        name="gla",
    )(proj4, proj4, proj4, proj4, proj4, state_t, w2p, gbias.reshape(1, kw), gn.reshape(1, B_DV))


def _xattn_kernel(q_ref, k_ref, v_ref, gq_ref, gk_ref, o_ref, *rest, bb, norm_k, token_major):
    gq, gk = gq_ref[...], gk_ref[...]
    scale = C_HD ** -0.5
    for i in range(bb):
        q = q_ref[i]
        outs, kns = [], []
        for h in range(C_HEADS):
            hs = slice(C_HD * h, C_HD * (h + 1))
            if token_major:
                rows = pl.ds(h, N_MEM, stride=C_HEADS)
                kh, vh = k_ref[i, rows, :], v_ref[i, rows, :]
            else:
                kh, vh = k_ref[i, :, hs], v_ref[i, :, hs]
            if norm_k:
                kh = _rms(kh, gk)
                kns.append(kh)
            qh = _rms(q[:, hs], gq)
            s = lax.dot_general(qh.astype(BF16), kh.astype(BF16), NT_DIMS, preferred_element_type=F32) * scale
            p = jnp.exp(s - jnp.max(s, axis=-1, keepdims=True))
            p = p / jnp.sum(p, axis=-1, keepdims=True)
            outs.append(jnp.dot(p.astype(BF16), vh.astype(BF16), preferred_element_type=F32))
        o_ref[i] = jnp.concatenate(outs, axis=-1)
        if norm_k:
            rest[0][i] = jnp.concatenate(kns, axis=-1)


def xattn(proj3, mem_k, mem_v_arr, v_col, gq, gk, bb, tq, norm_k, token_major=False):
    batch, t, _ = proj3.shape
    cw = C_HEADS * C_HD
    whole = lambda *shape: pl.BlockSpec(shape, lambda b, i: (0,) * len(shape))
    out_specs = [pl.BlockSpec((bb, tq, cw), lambda b, i: (b, i, 0))]
    out_shape = [jax.ShapeDtypeStruct((batch, t, cw), F32)]
    if norm_k:
        out_specs.append(pl.BlockSpec((bb, N_MEM, cw), lambda b, i: (b, 0, 0)))
        out_shape.append(jax.ShapeDtypeStruct((batch, N_MEM, cw), F32))
    if token_major:
        kspec = vspec = pl.BlockSpec((bb, N_MEM * C_HEADS, C_HD), lambda b, i: (b, 0, 0))
    else:
        kspec = pl.BlockSpec((bb, N_MEM, cw), lambda b, i: (b, 0, 0))
        vspec = pl.BlockSpec((bb, N_MEM, cw), lambda b, i: (b, 0, v_col))
    return pl.pallas_call(
        functools.partial(_xattn_kernel, bb=bb, norm_k=norm_k, token_major=token_major),
        grid=(batch // bb, t // tq),
        in_specs=[pl.BlockSpec((bb, tq, cw), lambda b, i: (b, i, COL_CQ // cw)), kspec, vspec,
                  whole(1, C_HD), whole(1, C_HD)],
        out_specs=out_specs,
        out_shape=out_shape,
        compiler_params=_params("arbitrary", "arbitrary"),
        name="xattn",
    )(proj3, mem_k, mem_v_arr, gq.reshape(1, C_HD), gk.reshape(1, C_HD))


def _merge_kernel(oa_ref, ob_ref, oc_ref, gl_ref, bg_ref, x_ref, wa_ref, wb_ref, wc_ref, wo_ref, g2_ref,
                  h_ref, xn_ref):
    d = D_MODEL
    gate = jax.nn.sigmoid(gl_ref[...] + bg_ref[...])
    pa = jnp.dot(oa_ref[...].astype(BF16), wa_ref[...], preferred_element_type=F32)
    pb = jnp.dot(ob_ref[...].astype(BF16), wb_ref[...], preferred_element_type=F32)
    pc = jnp.dot(oc_ref[...].astype(BF16), wc_ref[...], preferred_element_type=F32)
    merged = gate[:, :d] * pa + gate[:, d:2 * d] * pb + gate[:, 2 * d:] * pc
    h = x_ref[...] + jnp.dot(merged.astype(BF16), wo_ref[...], preferred_element_type=F32)
    h_ref[...] = h
    xn_ref[...] = _rms(h, g2_ref[...]).astype(BF16)


def merge(oa, ob, oc, proj, b_gates, x, wa, wb, wc, wo, g2, tm):
    n, d = x.shape
    bw = oa.shape[1]
    tok = lambda w: pl.BlockSpec((tm, w), lambda i: (i, 0))
    whole = lambda *shape: pl.BlockSpec(shape, lambda i: (0,) * len(shape))
    return pl.pallas_call(
        _merge_kernel,
        grid=(n // tm,),
        in_specs=[tok(bw), tok(bw), tok(bw), tok(3 * d), whole(1, 3 * d), tok(d),
                  whole(bw, d), whole(bw, d), whole(bw, d), whole(d, d), whole(1, d)],
        out_specs=[tok(d), tok(d)],
        out_shape=[jax.ShapeDtypeStruct((n, d), F32), jax.ShapeDtypeStruct((n, d), BF16)],
        compiler_params=_params("arbitrary"),
        name="merge",
    )(oa, ob, oc, proj, b_gates.reshape(1, 3 * d), x, wa, wb, wc, wo, g2.reshape(1, d))


_UNRANKED = float(P_NKEYS)


def _top16_exact(s):
    nrow = s.shape[0]
    row = lax.broadcasted_iota(I32, s.shape, 0)
    rank = jnp.full(s.shape, _UNRANKED, F32)
    vals = []
    for kk in range(P_TOPK):
        m = jnp.max(s, axis=0, keepdims=True)
        idx = jnp.min(jnp.where(s == m, row, nrow), axis=0, keepdims=True)
        hit = row == idx
        rank = jnp.where(hit, float(kk), rank)
        s = jnp.where(hit, -jnp.inf, s)
        vals.append(m)
    return jnp.concatenate(vals, axis=0), rank


def _top16_distinct(s, want_rank=True):
    rank = jnp.zeros(s.shape, F32)
    m = jnp.max(s, axis=0, keepdims=True)
    vals = [m]
    for kk in range(1, P_TOPK):
        below = s < m
        if want_rank:
            rank = jnp.where(below, float(kk), rank)
        m = jnp.max(jnp.where(below, s, -jnp.inf), axis=0, keepdims=True)
        vals.append(m)
    below = s < m
    rank = jnp.where(below, _UNRANKED, rank)
    count = jnp.sum(jnp.where(below, 0.0, 1.0), axis=0, keepdims=True)
    return jnp.concatenate(vals, axis=0), rank, below, count == float(P_TOPK)


_CAND_ROWS = 16 + 7 * 8 + 8


def _candidates(v1, v2):
    blocks = [v1[0:1] + v2[0:16]]
    blocks += [v1[a:a + 1] + v2[0:8] for a in range(1, 8)]
    blocks.append(v1[8:16] + v2[0:1])
    return jnp.concatenate(blocks, axis=0)


def _route_store(refs, ls, s1, s2, v1, rank1, v2, rank2, cand, sel):
    nb_ref, e1_ref, r2_ref, e2_ref = refs
    cmax = v1[0:1] + v2[0:1]
    zsum = jnp.sum(jnp.where(sel, jnp.exp(cand - cmax), 0.0), axis=0, keepdims=True)
    self32 = jnp.where(sel, 1.0, 0.0)
    nb = jnp.zeros(s1.shape, F32)
    nb = jnp.where(rank1 == 0.0, jnp.sum(self32[0:16], axis=0, keepdims=True), nb)
    for a in range(1, 8):
        cnt = jnp.sum(self32[16 + 8 * (a - 1):16 + 8 * a], axis=0, keepdims=True)
        nb = jnp.where(rank1 == float(a), cnt, nb)
    for a in range(8, 16):
        nb = jnp.where(rank1 == float(a), self32[72 + a - 8:73 + a - 8], nb)
    nb_ref[0, :, ls] = nb
    e1_ref[0, :, ls] = jnp.where(rank1 < float(P_TOPK), jnp.exp(s1 - v1[0:1]), 0.0)
    e2_ref[0, :, ls] = jnp.where(rank2 < float(P_TOPK), jnp.exp(s2 - v2[0:1]) * (0.5 / zsum), 0.0)
    r2_ref[0, :, ls] = pltpu.bitcast(rank2.astype(BF16), jnp.uint32)


def _peer_route_kernel(xn_ref, wq_ref, k1_ref, k2_ref, nb_ref, e1_ref, r2_ref, e2_ref, s_ref):
    tb = xn_ref.shape[0]
    refs = (nb_ref, e1_ref, r2_ref, e2_ref)
    qt = lax.dot_general(wq_ref[...], xn_ref[...], NT_DIMS, preferred_element_type=F32)
    half = qt.shape[0] // 2
    s_ref[0] = jnp.dot(k1_ref[...], qt[:half].astype(BF16), preferred_element_type=F32)
    s_ref[1] = jnp.dot(k2_ref[...], qt[half:].astype(BF16), preferred_element_type=F32)
    strips = [slice(LANE * st, LANE * (st + 1)) for st in range(tb // LANE)]
    bad = []
    for ls in strips:
        s1 = s_ref[0, :, ls]
        s2 = s_ref[1, :, ls]
        v1, rank1, _, ok1 = _top16_distinct(s1)
        v2, rank2, _, ok2 = _top16_distinct(s2)
        cand = _candidates(v1, v2)
        _, _, cbelow, okc = _top16_distinct(cand, want_rank=False)
        _route_store(refs, ls, s1, s2, v1, rank1, v2, rank2, cand, jnp.logical_not(cbelow))
        ok = jnp.logical_and(jnp.logical_and(ok1, ok2), okc)
        bad.append(jnp.max(jnp.where(ok, 0.0, 1.0)) > 0.5)

    for ls, bad_strip in zip(strips, bad):
        @pl.when(bad_strip)
        def _(ls=ls):
            s1 = s_ref[0, :, ls]
            s2 = s_ref[1, :, ls]
            v1, rank1 = _top16_exact(s1)
            v2, rank2 = _top16_exact(s2)
            cand = _candidates(v1, v2)
            _, crank = _top16_exact(cand)
            _route_store(refs, ls, s1, s2, v1, rank1, v2, rank2, cand, crank < float(P_TOPK))


def peer_route(xn, wq_t, k1, k2, tb):
    n, d = xn.shape
    hw = wq_t.shape[0] // P_HEADS
    ospec = pl.BlockSpec((1, P_NKEYS, tb), lambda i, h: (h, 0, i))
    pspec = pl.BlockSpec((1, P_NKEYS // 2, tb), lambda i, h: (h, 0, i))
    shape = lambda dt: jax.ShapeDtypeStruct((P_HEADS, P_NKEYS, n), dt)
    packed = jax.ShapeDtypeStruct((P_HEADS, P_NKEYS // 2, n), jnp.uint32)
    return pl.pallas_call(
        _peer_route_kernel,
        grid=(n // tb, P_HEADS),
        in_specs=[pl.BlockSpec((tb, d), lambda i, h: (i, 0)),
                  pl.BlockSpec((hw, d), lambda i, h: (h, 0)),
                  pl.BlockSpec((P_NKEYS, hw // 2), lambda i, h: (0, 0)),
                  pl.BlockSpec((P_NKEYS, hw // 2), lambda i, h: (0, 0))],
        out_specs=[ospec, ospec, pspec, ospec],
        out_shape=[shape(F32), shape(F32), packed, shape(F32)],
        scratch_shapes=[pltpu.VMEM((2, P_NKEYS, tb), F32)],
        compiler_params=_params("arbitrary", "arbitrary"),
        name="peer_route",
    )(xn, wq_t, k1, k2)


MXU_DIM = 256


BF16_ROWS = 16
C_GROUP = 2


def _peer_dense_kernel(xn_ref, u_ref, vt_ref, nb_ref, e1_ref, r2_ref, e2_ref, h_ref, y_ref,
                       acc_ref, hid_ref, act_ref, e2b_ref):
    j = pl.program_id(1)
    tb = xn_ref.shape[0]
    eb = u_ref.shape[0]
    n1 = eb // P_NKEYS
    nrg = P_NKEYS // BF16_ROWS
    half = BF16_ROWS // 2

    @pl.when(j == 0)
    def _():
        acc_ref[...] = jnp.zeros_like(acc_ref)
        for h in range(P_HEADS):
            e2b_ref[h] = e2_ref[h].astype(BF16)

    hid_ref[...] = lax.dot_general(u_ref[...], xn_ref[...], NT_DIMS, preferred_element_type=F32)
    zero = jnp.zeros((), BF16)

    def row_vreg(ref, c, h, ls):
        row = ref[pl.ds(h * P_NKEYS + j * n1 + c, 1), :]
        return jnp.broadcast_to(row[:, ls], (BF16_ROWS, LANE)).astype(BF16)

    for st in range(tb // LANE):
        ls = slice(LANE * st, LANE * (st + 1))
        for c0 in range(0, n1, C_GROUP):
            nbv = [[row_vreg(nb_ref, c0 + k, h, ls) for h in range(P_HEADS)] for k in range(C_GROUP)]
            e1v = [[row_vreg(e1_ref, c0 + k, h, ls) for h in range(P_HEADS)] for k in range(C_GROUP)]
            for rg in range(nrg):
                ws = slice(half * rg, half * (rg + 1))
                gs = [jnp.zeros((BF16_ROWS, LANE), BF16) for _ in range(C_GROUP)]
                for h in range(P_HEADS):
                    r2 = pltpu.bitcast(r2_ref[h, ws, ls], BF16)
                    e2 = e2b_ref[h, BF16_ROWS * rg:BF16_ROWS * (rg + 1), ls]
                    for k in range(C_GROUP):
                        gs[k] = gs[k] + jnp.where(r2 < nbv[k][h], e2, zero) * e1v[k][h]
                for k in range(C_GROUP):
                    rs = slice(P_NKEYS * (c0 + k) + BF16_ROWS * rg, P_NKEYS * (c0 + k) + BF16_ROWS * (rg + 1))
                    hh = hid_ref[rs, ls]
                    gelu2 = hh * (1.0 + lax.erf(hh * math.sqrt(0.5)))
                    act_ref[rs, ls] = gelu2.astype(BF16) * gs[k]
    acc_ref[...] += jnp.dot(vt_ref[...], act_ref[...], preferred_element_type=F32)

    @pl.when(j == pl.num_programs(1) - 1)
    def _():
        y_ref[...] = h_ref[...] + acc_ref[...].T


def peer_dense(xn, u, vt, nb, e1, r2, e2, h, tb, eb):
    n, d = xn.shape
    ne = u.shape[0]
    rspec = pl.BlockSpec((P_HEADS * P_NKEYS, tb), lambda i, j: (0, i))
    pspec = pl.BlockSpec((P_HEADS, P_NKEYS // 2, tb), lambda i, j: (0, 0, i))
    nb = nb.reshape(P_HEADS * P_NKEYS, n)
    e1 = e1.reshape(P_HEADS * P_NKEYS, n)
    return pl.pallas_call(
        _peer_dense_kernel,
        grid=(n // tb, ne // eb),
        in_specs=[pl.BlockSpec((tb, d), lambda i, j: (i, 0)),
                  pl.BlockSpec((eb, d), lambda i, j: (j, 0)),
                  pl.BlockSpec((d, eb), lambda i, j: (0, j)),
                  rspec, rspec, pspec, pl.BlockSpec((P_HEADS, P_NKEYS, tb), lambda i, j: (0, 0, i)),
                  pl.BlockSpec((tb, d), lambda i, j: (i, 0))],
        out_specs=pl.BlockSpec((tb, d), lambda i, j: (i, 0)),
        out_shape=jax.ShapeDtypeStruct((n, d), F32),
        scratch_shapes=[pltpu.VMEM((d, tb), F32), pltpu.VMEM((eb, tb), F32), pltpu.VMEM((eb, tb), BF16),
                        pltpu.VMEM((P_HEADS, P_NKEYS, tb), BF16)],
        compiler_params=_params("arbitrary", "arbitrary"),
        name="peer_dense",
    )(xn, u, vt, nb, e1, r2, e2, h)


TOKEN_TILE = 512
PEER_EXPERT_TILE = 8 * P_NKEYS
SAMPLE_BATCH_TILE = 8


def _permute_w_in(w_in):
    d = w_in.shape[0]
    seg = lambda a, b: w_in[:, a:b]
    lr = jnp.pad(seg(1792, 1808), ((0, 0), (0, LANE - B_RANK)))
    cols = [seg(2832, 5904), seg(0, 512), seg(1280, 1792), seg(1808, 2320), seg(2320, 2832), seg(768, 1024),
            seg(1024, 1280), seg(512, 640), seg(640, 768), lr, jnp.zeros((d, IN_COLS - COL_LR - LANE), w_in.dtype)]
    return jnp.concatenate(cols, axis=1).astype(BF16)


def kernel(x_prompt, x_sample, mem_prompt, cache_win_k, cache_win_v, state_gla, cache_mem_k, cache_mem_v, norm1_g, w_in, b_gates, a_qnorm_g, a_knorm_g, a_sinks, rel_bias, b_gate_w2, b_gate_b, b_gn_g, c_qnorm_g, c_knorm_g, mem_norm_g, w_mem_kv, w_br_a, w_br_b, w_br_c, w_out, norm2_g, peer_wq, peer_k1, peer_k2, peer_u, peer_v):
    depth = w_in.shape[0]
    assert depth == 1, "one layer"
    l = 0
    bp, tp, d = x_prompt.shape
    bs, ts, _ = x_sample.shape
    cw = C_HEADS * C_HD
    kvw = A_KV_HEADS * A_HD

    w_all = _permute_w_in(w_in[l])
    w2p = jnp.pad(b_gate_w2[l], ((0, LANE - B_RANK), (0, 0))).astype(BF16)
    wa, wb, wc, wo = (w.astype(BF16) for w in (w_br_a[l], w_br_b[l], w_br_c[l], w_out[l]))
    wq_t = peer_wq[l].T.astype(BF16)
    k1, k2 = peer_k1[l].astype(BF16), peer_k2[l].astype(BF16)
    u = peer_u[l].astype(BF16)
    vt = peer_v[l].T.astype(BF16)

    def channel_mix(oa, ob, oc, proj, x2):
        tm = min(TOKEN_TILE, x2.shape[0])
        h, xn2 = merge(oa, ob, oc, proj, b_gates[l], x2, wa, wb, wc, wo, norm2_g[l], tm)
        nb, e1, r2, e2 = peer_route(xn2, wq_t, k1, k2, tm)
        return peer_dense(xn2, u, vt, nb, e1, r2, e2, h, tm, PEER_EXPERT_TILE)

    xp = x_prompt.reshape(bp * tp, d)
    proj_p = norm_matmul(xp, norm1_g[l], w_all, min(TOKEN_TILE, bp * tp), IN_COLS // 2)
    memkv = norm_matmul(mem_prompt.reshape(bp * N_MEM, d), mem_norm_g[l], w_mem_kv[l].astype(BF16), min(TOKEN_TILE, bp * N_MEM), 2 * cw)
    oa_p, wk_p, wv_p = swa_prompt(proj_p, bp, tp, rel_bias, a_qnorm_g[l], a_knorm_g[l], a_sinks[l])
    nchunk = tp // B_CHUNK
    ob_p, st_p = gla(proj_p.reshape(bp, nchunk, B_CHUNK, IN_COLS), jnp.zeros((bp, B_HEADS, B_DV, B_DK), F32),
                     bp, w2p, b_gate_b[l], b_gn_g[l])
    memkv3 = memkv.reshape(bp, N_MEM, 2 * cw)
    oc_p, mk_p = xattn(proj_p.reshape(bp, tp, IN_COLS), memkv3, memkv3, 1, c_qnorm_g[l], c_knorm_g[l],
                       1, min(TOKEN_TILE, tp), True)
    y_p = channel_mix(oa_p, ob_p.reshape(bp * tp, -1), oc_p.reshape(bp * tp, cw), proj_p, xp)

    xs = x_sample.reshape(bs * ts, d)
    proj_s = norm_matmul(xs, norm1_g[l], w_all, min(TOKEN_TILE, bs * ts), IN_COLS // 2)
    proj_s3 = proj_s.reshape(bs, ts, IN_COLS)
    ck = cache_win_k[l].reshape(bs, -1, kvw)
    cv = cache_win_v[l].reshape(bs, -1, kvw)
    oa_s, knew = swa_sample(proj_s3, ck, cv, rel_bias, a_qnorm_g[l], a_knorm_g[l], a_sinks[l], SAMPLE_BATCH_TILE)
    vnew = proj_s3[:, :, COL_AV:COL_AV + kvw]
    ob_s, st_s = gla(proj_s3.reshape(bs, 1, ts, IN_COLS), jnp.swapaxes(state_gla[l], -1, -2),
                     math.gcd(bs, SAMPLE_BATCH_TILE), w2p, b_gate_b[l], b_gn_g[l])
    oc_s, = xattn(proj_s3, cache_mem_k[l].reshape(bs, N_MEM * C_HEADS, C_HD),
                  cache_mem_v[l].reshape(bs, N_MEM * C_HEADS, C_HD), 0,
                  c_qnorm_g[l], c_knorm_g[l], SAMPLE_BATCH_TILE, ts, False, token_major=True)
    y_s = channel_mix(oa_s.reshape(bs * ts, -1), ob_s.reshape(bs * ts, -1), oc_s.reshape(bs * ts, cw), proj_s, xs)

    wc_len = ck.shape[1]
    new_wk_s = jnp.concatenate([ck, knew], axis=1)[:, -wc_len:]
    new_wv_s = jnp.concatenate([cv, vnew], axis=1)[:, -wc_len:]
    return (y_p.reshape(bp, tp, d), y_s.reshape(bs, ts, d),
            wk_p.reshape(1, bp, WINDOW, A_KV_HEADS, A_HD), wv_p.reshape(1, bp, WINDOW, A_KV_HEADS, A_HD),
            jnp.swapaxes(st_p, -1, -2)[None],
            mk_p.reshape(1, bp, N_MEM, C_HEADS, C_HD), memkv3[:, :, cw:].reshape(1, bp, N_MEM, C_HEADS, C_HD),
            new_wk_s.reshape(1, bs, wc_len, A_KV_HEADS, A_HD), new_wv_s.reshape(1, bs, wc_len, A_KV_HEADS, A_HD),
            jnp.swapaxes(st_s, -1, -2)[None])
```

```python
import functools
import math

import numpy as np
import jax
import jax.numpy as jnp
from jax import lax
from jax.experimental import pallas as pl
from jax.experimental.pallas import tpu as pltpu

F32 = jnp.float32
BF16 = jnp.bfloat16
I32 = jnp.int32

EPS = 1e-6
D_MODEL = 1024
N_MEM = 256
A_HEADS, A_KV_HEADS, A_GROUP, A_HD = 8, 2, 4, 64
WINDOW = 128
N_BUCKETS, MAX_DISTANCE = 32, 128
B_HEADS, B_DK, B_DV, B_RANK, B_TAU = 4, 64, 128, 16, 16.0
B_CHUNK, B_SUB = 64, 16
C_HEADS, C_HD = 4, 128
P_HEADS, P_NKEYS, P_TOPK = 8, 128, 16
P_EXPERTS = P_NKEYS * P_NKEYS

LANE = 128
COL_GL, COL_AQ, COL_BV, COL_BR, COL_CQ, COL_BQ, COL_BK, COL_AK, COL_AV, COL_LR = (
    0, 3072, 3584, 4096, 4608, 5120, 5376, 5632, 5760, 5888)
IN_COLS = 6144
VMEM_LIMIT = 56 * 1024 * 1024

NT_DIMS = (((1,), (1,)), ((), ()))
TN_DIMS = (((0,), (0,)), ((), ()))


def _rms(x, g):
    ms = jnp.mean(x * x, axis=-1, keepdims=True)
    return x * lax.rsqrt(ms + EPS) * g


def _params(*sem, flags=None):
    return pltpu.CompilerParams(dimension_semantics=sem, vmem_limit_bytes=VMEM_LIMIT, flags=flags)


def _norm_matmul_kernel(x_ref, g_ref, w_ref, o_ref):
    xn = _rms(x_ref[...], g_ref[...]).astype(BF16)
    o_ref[...] = jnp.dot(xn, w_ref[...], preferred_element_type=F32)


def norm_matmul(x, g, w, tm, tn):
    n, d = x.shape
    m = w.shape[1]
    return pl.pallas_call(
        _norm_matmul_kernel,
        grid=(m // tn, n // tm),
        in_specs=[pl.BlockSpec((tm, d), lambda j, i: (i, 0)),
                  pl.BlockSpec((1, d), lambda j, i: (0, 0)),
                  pl.BlockSpec((d, tn), lambda j, i: (0, j))],
        out_specs=pl.BlockSpec((tm, tn), lambda j, i: (i, j)),
        out_shape=jax.ShapeDtypeStruct((n, m), F32),
        compiler_params=_params("arbitrary", "arbitrary"),
        name="norm_matmul",
    )(x, g.reshape(1, d), w)


def _t5_bucket_np(dist):
    n = np.maximum(dist, 0)
    max_exact = N_BUCKETS // 2
    nf = np.maximum(n, 1).astype(np.float32)
    large = max_exact + (np.log(nf / np.float32(max_exact)) / np.float32(math.log(MAX_DISTANCE / max_exact))
                         * np.float32(N_BUCKETS - max_exact)).astype(np.int32)
    large = np.minimum(large, N_BUCKETS - 1)
    return np.where(n < max_exact, n, large).astype(np.int32)


def _bias_from_buckets(bucket, rb_ref, head):
    tab = jnp.zeros(bucket.shape, F32)
    for bk in range(N_BUCKETS):
        tab = jnp.where(bucket == bk, rb_ref[bk, head], tab)
    return tab


def _swa_prompt_kernel(rb_ref, q_ref, kc_ref, kp_ref, vc_ref, vp_ref, gq_ref, gk_ref, snk_ref, bucket_ref,
                       o_ref, wk_ref, wv_ref, bias_ref):
    blk = WINDOW
    first = jnp.logical_and(pl.program_id(0) == 0, pl.program_id(1) == 0)

    @pl.when(first)
    def _():
        bucket = bucket_ref[...]
        qi = lax.broadcasted_iota(I32, (blk, 2 * blk), 0)
        kj = lax.broadcasted_iota(I32, (blk, 2 * blk), 1)
        dist = qi + blk - kj
        band = jnp.logical_and(dist >= 0, dist < WINDOW)
        for h in range(A_HEADS):
            bias_ref[h] = jnp.where(band, _bias_from_buckets(bucket, rb_ref, h), -jnp.inf)

    i = pl.program_id(1)
    nsub = q_ref.shape[0] // blk
    gq, gk = gq_ref[...], gk_ref[...]
    col = lax.broadcasted_iota(I32, (A_GROUP * blk, 2 * blk), 1)
    keep_first = jnp.logical_or(col >= blk, i > 0)
    scale = A_HD ** -0.5
    ones = jnp.ones((2 * blk, A_HD), BF16)
    kn = [[_rms(kp_ref[:, A_HD * g:A_HD * (g + 1)], gk)] +
          [_rms(kc_ref[blk * sb:blk * (sb + 1), A_HD * g:A_HD * (g + 1)], gk) for sb in range(nsub)]
          for g in range(A_KV_HEADS)]
    vv = [[vp_ref[:, A_HD * g:A_HD * (g + 1)]] +
          [vc_ref[blk * sb:blk * (sb + 1), A_HD * g:A_HD * (g + 1)] for sb in range(nsub)]
          for g in range(A_KV_HEADS)]
    for sb in range(nsub):
        q = q_ref[blk * sb:blk * (sb + 1), :]
        outs = [None] * A_HEADS
        for g in range(A_KV_HEADS):
            kcat = jnp.concatenate([kn[g][sb], kn[g][sb + 1]], axis=0).astype(BF16)
            vext = jnp.concatenate([jnp.concatenate([vv[g][sb], vv[g][sb + 1]], axis=0).astype(BF16), ones], axis=1)
            qg = jnp.concatenate(
                [_rms(q[:, A_HD * (A_GROUP * g + j):A_HD * (A_GROUP * g + j + 1)], gq) for j in range(A_GROUP)],
                axis=0)
            s = lax.dot_general(qg.astype(BF16), kcat, NT_DIMS, preferred_element_type=F32) * scale
            s = s + bias_ref[A_GROUP * g:A_GROUP * (g + 1)].reshape(A_GROUP * blk, 2 * blk)
            if sb == 0:
                s = jnp.where(keep_first, s, -jnp.inf)
            snk = snk_ref[g]
            m = jnp.maximum(jnp.max(s, axis=-1, keepdims=True), snk)
            p = jnp.exp(s - m)
            oext = jnp.dot(p.astype(BF16), vext, preferred_element_type=F32)
            den = oext[:, A_HD:A_HD + 1] + jnp.exp(snk - m)
            o = oext[:, :A_HD] / den
            for j in range(A_GROUP):
                outs[A_GROUP * g + j] = o[blk * j:blk * (j + 1)]
        o_ref[blk * sb:blk * (sb + 1), :] = jnp.concatenate(outs, axis=-1)
    wk_ref[0] = jnp.concatenate([kn[g][nsub] for g in range(A_KV_HEADS)], axis=-1)
    wv_ref[0] = vc_ref[blk * (nsub - 1):blk * nsub, :]


SWA_QUERY_BLOCKS = 4


def swa_prompt(proj, batch, seq, rel_bias, gq, gk, sinks):
    blk = WINDOW
    nsub = math.gcd(SWA_QUERY_BLOCKS, seq // blk)
    tq = nsub * blk
    nb = seq // tq
    qi = np.arange(blk)[:, None]
    kj = np.arange(2 * blk)[None, :]
    bucket = jnp.asarray(_t5_bucket_np(qi + blk - kj))
    snk = jnp.broadcast_to(sinks.astype(F32).reshape(A_KV_HEADS, A_GROUP, 1, 1),
                           (A_KV_HEADS, A_GROUP, blk, 1)).reshape(A_KV_HEADS, A_GROUP * blk, 1)
    cur = lambda c: (lambda b, i: (b * nb + i, c))
    prev = lambda c: (lambda b, i: (jnp.maximum((b * nb + i) * nsub - 1, b * nb * nsub), c))
    kvw = A_KV_HEADS * A_HD
    return pl.pallas_call(
        _swa_prompt_kernel,
        grid=(batch, nb),
        in_specs=[pl.BlockSpec(memory_space=pltpu.SMEM),
                  pl.BlockSpec((tq, A_HEADS * A_HD), cur(COL_AQ // (A_HEADS * A_HD))),
                  pl.BlockSpec((tq, kvw), cur(COL_AK // kvw)),
                  pl.BlockSpec((blk, kvw), prev(COL_AK // kvw)),
                  pl.BlockSpec((tq, kvw), cur(COL_AV // kvw)),
                  pl.BlockSpec((blk, kvw), prev(COL_AV // kvw)),
                  pl.BlockSpec((1, A_HD), lambda b, i: (0, 0)),
                  pl.BlockSpec((1, A_HD), lambda b, i: (0, 0)),
                  pl.BlockSpec((A_KV_HEADS, A_GROUP * blk, 1), lambda b, i: (0, 0, 0)),
                  pl.BlockSpec((blk, 2 * blk), lambda b, i: (0, 0))],
        out_specs=[pl.BlockSpec((tq, A_HEADS * A_HD), lambda b, i: (b * nb + i, 0)),
                   pl.BlockSpec((1, blk, kvw), lambda b, i: (b, 0, 0)),
                   pl.BlockSpec((1, blk, kvw), lambda b, i: (b, 0, 0))],
        out_shape=[jax.ShapeDtypeStruct((batch * seq, A_HEADS * A_HD), F32),
                   jax.ShapeDtypeStruct((batch, blk, kvw), F32),
                   jax.ShapeDtypeStruct((batch, blk, kvw), F32)],
        scratch_shapes=[pltpu.VMEM((A_HEADS, blk, 2 * blk), F32)],
        compiler_params=_params("arbitrary", "arbitrary"),
        name="swa_prompt",
    )(rel_bias.astype(F32), proj, proj, proj, proj, proj, gq.reshape(1, A_HD), gk.reshape(1, A_HD), snk, bucket)


def _swa_sample_kernel(rb_ref, snk_ref, q_ref, kn_ref, vn_ref, ck_ref, cv_ref, gq_ref, gk_ref, bkt_c_ref, bkt_n_ref,
                       o_ref, knew_ref, bias_c_ref, bias_n_ref, *, t_new):
    wc = WINDOW

    @pl.when(pl.program_id(0) == 0)
    def _():
        t = lax.broadcasted_iota(I32, (t_new, LANE), 0)
        c = lax.broadcasted_iota(I32, (t_new, LANE), 1)
        valid_c = c > t
        valid_n = c <= t
        for h in range(A_HEADS):
            bias_c_ref[h] = jnp.where(valid_c, _bias_from_buckets(bkt_c_ref[...], rb_ref, h), -jnp.inf)
            bias_n_ref[h] = jnp.where(valid_n, _bias_from_buckets(bkt_n_ref[...], rb_ref, h), -jnp.inf)

    q = q_ref[...]
    gq, gk = gq_ref[...], gk_ref[...]
    scale = A_HD ** -0.5
    kn_all = []
    kv = []
    for g in range(A_KV_HEADS):
        ks = slice(A_HD * g, A_HD * (g + 1))
        knn = _rms(kn_ref[:, :, ks], gk)
        kn_all.append(knn)
        kv.append((ck_ref[:, :, ks].astype(BF16), cv_ref[:, :, ks].astype(BF16), knn, vn_ref[:, :, ks]))
    outs = []
    for h in range(A_HEADS):
        kc, vc, knn, vnn = kv[h // A_GROUP]
        qh = _rms(q[:, :, A_HD * h:A_HD * (h + 1)], gq)
        s_c = jnp.einsum('bqd,bkd->bqk', qh.astype(BF16), kc, preferred_element_type=F32) * scale + bias_c_ref[h]
        s_n = [jnp.sum(qh * knn[:, j:j + 1, :], axis=-1, keepdims=True) * scale + bias_n_ref[h][:, j:j + 1]
               for j in range(t_new)]
        snk = snk_ref[h]
        m = jnp.maximum(jnp.max(s_c, axis=-1, keepdims=True), snk)
        for j in range(t_new):
            m = jnp.maximum(m, s_n[j])
        p_c = jnp.exp(s_c - m)
        p_n = [jnp.exp(s_n[j] - m) for j in range(t_new)]
        den = jnp.sum(p_c, axis=-1, keepdims=True) + jnp.exp(snk - m)
        for j in range(t_new):
            den = den + p_n[j]
        o = jnp.einsum('bqk,bkd->bqd', (p_c / den).astype(BF16), vc, preferred_element_type=F32)
        for j in range(t_new):
            o = o + (p_n[j] / den) * vnn[:, j:j + 1, :]
        outs.append(o)
    o_ref[...] = jnp.concatenate(outs, axis=-1)
    knew_ref[...] = jnp.concatenate(kn_all, axis=-1)


def swa_sample(proj3, cache_k, cache_v, rel_bias, gq, gk, sinks, bb):
    batch, t_new, _ = proj3.shape
    wc = cache_k.shape[1]
    kvw = A_KV_HEADS * A_HD
    t = np.arange(t_new)[:, None]
    c = np.arange(LANE)[None, :]
    bkt_c = jnp.asarray(_t5_bucket_np(wc + t - c))
    bkt_n = jnp.asarray(_t5_bucket_np(t - c))
    whole = lambda *shape: pl.BlockSpec(shape, lambda i: (0,) * len(shape))
    return pl.pallas_call(
        functools.partial(_swa_sample_kernel, t_new=t_new),
        grid=(batch // bb,),
        in_specs=[pl.BlockSpec(memory_space=pltpu.SMEM),
                  pl.BlockSpec(memory_space=pltpu.SMEM),
                  pl.BlockSpec((bb, t_new, A_HEADS * A_HD), lambda i: (i, 0, COL_AQ // (A_HEADS * A_HD))),
                  pl.BlockSpec((bb, t_new, kvw), lambda i: (i, 0, COL_AK // kvw)),
                  pl.BlockSpec((bb, t_new, kvw), lambda i: (i, 0, COL_AV // kvw)),
                  pl.BlockSpec((bb, wc, kvw), lambda i: (i, 0, 0)),
                  pl.BlockSpec((bb, wc, kvw), lambda i: (i, 0, 0)),
                  whole(1, A_HD), whole(1, A_HD), whole(t_new, LANE), whole(t_new, LANE)],
        out_specs=[pl.BlockSpec((bb, t_new, A_HEADS * A_HD), lambda i: (i, 0, 0)),
                   pl.BlockSpec((bb, t_new, kvw), lambda i: (i, 0, 0))],
        out_shape=[jax.ShapeDtypeStruct((batch, t_new, A_HEADS * A_HD), F32),
                   jax.ShapeDtypeStruct((batch, t_new, kvw), F32)],
        scratch_shapes=[pltpu.VMEM((A_HEADS, t_new, LANE), F32), pltpu.VMEM((A_HEADS, t_new, LANE), F32)],
        compiler_params=_params("arbitrary"),
        name="swa_sample",
    )(rel_bias.astype(F32), sinks.astype(F32), proj3, proj3, proj3, cache_k, cache_v,
      gq.reshape(1, A_HD), gk.reshape(1, A_HD), bkt_c, bkt_n)


def _gla_kernel(q_ref, k_ref, v_ref, lr_ref, r_ref, s0_ref, w2_ref, gb_ref, gn_ref, o_ref, sout_ref, st_ref,
                *, chunk, sub):
    c = pl.program_id(1)
    bb = q_ref.shape[0]

    @pl.when(c == 0)
    def _():
        st_ref[...] = s0_ref[...]

    gn = gn_ref[...]
    row = lax.broadcasted_iota(I32, (chunk, chunk), 0)
    col = lax.broadcasted_iota(I32, (chunk, chunk), 1)
    nsub = chunk // sub
    for i in range(bb):
        z = jnp.dot(lr_ref[i, 0].astype(BF16), w2_ref[...], preferred_element_type=F32) + gb_ref[...]
        b = (jnp.minimum(z, 0.0) - jnp.log1p(jnp.exp(-jnp.abs(z)))) / B_TAU
        row_w = lax.broadcasted_iota(I32, b.shape, 0)
        shift = 1
        while shift < chunk:
            b = b + jnp.where(row_w >= shift, pltpu.roll(b, shift, 0), 0.0)
            shift *= 2
        q = q_ref[i, 0] * (B_DK ** -0.5)
        k, v, r = k_ref[i, 0], v_ref[i, 0], r_ref[i, 0]
        outs = []
        for h in range(B_HEADS):
            ks = slice(B_DK * h, B_DK * (h + 1))
            vs = slice(B_DV * h, B_DV * (h + 1))
            qh, kh, vh, bh = q[:, ks], k[:, ks], v[:, vs], b[:, ks]
            st = st_ref[i, h]
            o = lax.dot_general((qh * jnp.exp(bh)).astype(BF16), st.astype(BF16), NT_DIMS,
                                preferred_element_type=F32)
            refs = [bh[sub * (j + 1) - 1:sub * (j + 1)] for j in range(nsub)]
            rfull = jnp.concatenate([jnp.broadcast_to(rj, (sub, B_DK)) for rj in refs], axis=0)
            kt = (kh * jnp.exp(rfull - bh)).astype(BF16)
            qcat = jnp.concatenate([qh * jnp.exp(jnp.minimum(bh - rj, 80.0)) for rj in refs], axis=0).astype(BF16)
            aa = lax.dot_general(qcat, kt, NT_DIMS, preferred_element_type=F32)
            att = jnp.zeros((chunk, chunk), F32)
            for j in range(nsub):
                in_sub = jnp.logical_and(col >= sub * j, col < sub * (j + 1))
                att = jnp.where(in_sub, aa[chunk * j:chunk * (j + 1)], att)
            att = jnp.where(col <= row, att, 0.0)
            o = o + jnp.dot(att.astype(BF16), vh.astype(BF16), preferred_element_type=F32)
            bl = bh[chunk - 1:chunk]
            kd = (kh * jnp.exp(bl - bh)).astype(BF16)
            st_ref[i, h] = st * jnp.exp(bl) + lax.dot_general(vh.astype(BF16), kd, TN_DIMS,
                                                              preferred_element_type=F32)
            rh = r[:, vs]
            outs.append(_rms(o, gn) * (rh * jax.nn.sigmoid(rh)))
        o_ref[i, 0] = jnp.concatenate(outs, axis=-1)
    sout_ref[...] = st_ref[...]


def gla(proj4, state_t, bb, w2p, gbias, gn):
    batch, nchunk, chunk, _ = proj4.shape
    sub = min(B_SUB, chunk)
    kw, vw = B_HEADS * B_DK, B_HEADS * B_DV
    blk = lambda w, col: pl.BlockSpec((bb, 1, chunk, w), lambda b, c: (b, c, 0, col // w))
    whole = lambda *shape: pl.BlockSpec(shape, lambda b, c: (0,) * len(shape))
    st_spec = pl.BlockSpec((bb, B_HEADS, B_DV, B_DK), lambda b, c: (b, 0, 0, 0))
    return pl.pallas_call(
        functools.partial(_gla_kernel, chunk=chunk, sub=sub),
        grid=(batch // bb, nchunk),
        in_specs=[blk(kw, COL_BQ), blk(kw, COL_BK), blk(vw, COL_BV), blk(LANE, COL_LR), blk(vw, COL_BR), st_spec,
                  whole(LANE, kw), whole(1, kw), whole(1, B_DV)],
        out_specs=[pl.BlockSpec((bb, 1, chunk, vw), lambda b, c: (b, c, 0, 0)), st_spec],
        out_shape=[jax.ShapeDtypeStruct((batch, nchunk, chunk, vw), F32),
                   jax.ShapeDtypeStruct((batch, B_HEADS, B_DV, B_DK), F32)],
        scratch_shapes=[pltpu.VMEM((bb, B_HEADS, B_DV, B_DK), F32)],
        compiler_params=_params("arbitrary", "arbitrary"),
        name="gla",
    )(proj4, proj4, proj4, proj4, proj4, state_t, w2p, gbias.reshape(1, kw), gn.reshape(1, B_DV))


def _xattn_kernel(q_ref, k_ref, v_ref, gq_ref, gk_ref, o_ref, *rest, bb, norm_k, token_major):
    gq, gk = gq_ref[...], gk_ref[...]
    scale = C_HD ** -0.5
    for i in range(bb):
        q = q_ref[i]
        outs, kns = [], []
        for h in range(C_HEADS):
            hs = slice(C_HD * h, C_HD * (h + 1))
            if token_major:
                rows = pl.ds(h, N_MEM, stride=C_HEADS)
                kh, vh = k_ref[i, rows, :], v_ref[i, rows, :]
            else:
                kh, vh = k_ref[i, :, hs], v_ref[i, :, hs]
            if norm_k:
                kh = _rms(kh, gk)
                kns.append(kh)
            qh = _rms(q[:, hs], gq)
            s = lax.dot_general(qh.astype(BF16), kh.astype(BF16), NT_DIMS, preferred_element_type=F32) * scale
            p = jnp.exp(s - jnp.max(s, axis=-1, keepdims=True))
            p = p / jnp.sum(p, axis=-1, keepdims=True)
            outs.append(jnp.dot(p.astype(BF16), vh.astype(BF16), preferred_element_type=F32))
        o_ref[i] = jnp.concatenate(outs, axis=-1)
        if norm_k:
            rest[0][i] = jnp.concatenate(kns, axis=-1)


def xattn(proj3, mem_k, mem_v_arr, v_col, gq, gk, bb, tq, norm_k, token_major=False):
    batch, t, _ = proj3.shape
    cw = C_HEADS * C_HD
    whole = lambda *shape: pl.BlockSpec(shape, lambda b, i: (0,) * len(shape))
    out_specs = [pl.BlockSpec((bb, tq, cw), lambda b, i: (b, i, 0))]
    out_shape = [jax.ShapeDtypeStruct((batch, t, cw), F32)]
    if norm_k:
        out_specs.append(pl.BlockSpec((bb, N_MEM, cw), lambda b, i: (b, 0, 0)))
        out_shape.append(jax.ShapeDtypeStruct((batch, N_MEM, cw), F32))
    if token_major:
        kspec = vspec = pl.BlockSpec((bb, N_MEM * C_HEADS, C_HD), lambda b, i: (b, 0, 0))
    else:
        kspec = pl.BlockSpec((bb, N_MEM, cw), lambda b, i: (b, 0, 0))
        vspec = pl.BlockSpec((bb, N_MEM, cw), lambda b, i: (b, 0, v_col))
    return pl.pallas_call(
        functools.partial(_xattn_kernel, bb=bb, norm_k=norm_k, token_major=token_major),
        grid=(batch // bb, t // tq),
        in_specs=[pl.BlockSpec((bb, tq, cw), lambda b, i: (b, i, COL_CQ // cw)), kspec, vspec,
                  whole(1, C_HD), whole(1, C_HD)],
        out_specs=out_specs,
        out_shape=out_shape,
        compiler_params=_params("arbitrary", "arbitrary"),
        name="xattn",
    )(proj3, mem_k, mem_v_arr, gq.reshape(1, C_HD), gk.reshape(1, C_HD))


def _merge_kernel(oa_ref, ob_ref, oc_ref, gl_ref, bg_ref, x_ref, wa_ref, wb_ref, wc_ref, wo_ref, g2_ref,
                  h_ref, xn_ref):
    d = D_MODEL
    gate = jax.nn.sigmoid(gl_ref[...] + bg_ref[...])
    pa = jnp.dot(oa_ref[...].astype(BF16), wa_ref[...], preferred_element_type=F32)
    pb = jnp.dot(ob_ref[...].astype(BF16), wb_ref[...], preferred_element_type=F32)
    pc = jnp.dot(oc_ref[...].astype(BF16), wc_ref[...], preferred_element_type=F32)
    merged = gate[:, :d] * pa + gate[:, d:2 * d] * pb + gate[:, 2 * d:] * pc
    h = x_ref[...] + jnp.dot(merged.astype(BF16), wo_ref[...], preferred_element_type=F32)
    h_ref[...] = h
    xn_ref[...] = _rms(h, g2_ref[...]).astype(BF16)


def merge(oa, ob, oc, proj, b_gates, x, wa, wb, wc, wo, g2, tm):
    n, d = x.shape
    bw = oa.shape[1]
    tok = lambda w: pl.BlockSpec((tm, w), lambda i: (i, 0))
    whole = lambda *shape: pl.BlockSpec(shape, lambda i: (0,) * len(shape))
    return pl.pallas_call(
        _merge_kernel,
        grid=(n // tm,),
        in_specs=[tok(bw), tok(bw), tok(bw), tok(3 * d), whole(1, 3 * d), tok(d),
                  whole(bw, d), whole(bw, d), whole(bw, d), whole(d, d), whole(1, d)],
        out_specs=[tok(d), tok(d)],
        out_shape=[jax.ShapeDtypeStruct((n, d), F32), jax.ShapeDtypeStruct((n, d), BF16)],
        compiler_params=_params("arbitrary"),
        name="merge",
    )(oa, ob, oc, proj, b_gates.reshape(1, 3 * d), x, wa, wb, wc, wo, g2.reshape(1, d))


_UNRANKED = float(P_NKEYS)


def _top16_exact(s):
    nrow = s.shape[0]
    row = lax.broadcasted_iota(I32, s.shape, 0)
    rank = jnp.full(s.shape, _UNRANKED, F32)
    vals = []
    for kk in range(P_TOPK):
        m = jnp.max(s, axis=0, keepdims=True)
        idx = jnp.min(jnp.where(s == m, row, nrow), axis=0, keepdims=True)
        hit = row == idx
        rank = jnp.where(hit, float(kk), rank)
        s = jnp.where(hit, -jnp.inf, s)
        vals.append(m)
    return jnp.concatenate(vals, axis=0), rank


def _top16_distinct(s, want_rank=True):
    rank = jnp.zeros(s.shape, F32)
    m = jnp.max(s, axis=0, keepdims=True)
    vals = [m]
    for kk in range(1, P_TOPK):
        below = s < m
        if want_rank:
            rank = jnp.where(below, float(kk), rank)
        m = jnp.max(jnp.where(below, s, -jnp.inf), axis=0, keepdims=True)
        vals.append(m)
    below = s < m
    rank = jnp.where(below, _UNRANKED, rank)
    count = jnp.sum(jnp.where(below, 0.0, 1.0), axis=0, keepdims=True)
    return jnp.concatenate(vals, axis=0), rank, below, count == float(P_TOPK)


_CAND_ROWS = 16 + 7 * 8 + 8


def _candidates(v1, v2):
    blocks = [v1[0:1] + v2[0:16]]
    blocks += [v1[a:a + 1] + v2[0:8] for a in range(1, 8)]
    blocks.append(v1[8:16] + v2[0:1])
    return jnp.concatenate(blocks, axis=0)


SUBLANES = 8


def _compare_exchange(v, i, l):
    v[i], v[l] = jnp.maximum(v[i], v[l]), jnp.minimum(v[i], v[l])


def _sorted_top16_slabs(s):
    nslab = s.shape[0] // SUBLANES
    assert nslab == P_TOPK
    v = [s[SUBLANES * i:SUBLANES * (i + 1)] for i in range(nslab)]
    k = 2
    while k <= nslab:
        j = k // 2
        while j >= 1:
            for i in range(nslab):
                l = i ^ j
                if l > i:
                    if (i & k) == 0:
                        _compare_exchange(v, i, l)
                    else:
                        _compare_exchange(v, l, i)
            j //= 2
        k *= 2
    shift = SUBLANES // 2
    while shift >= 1:
        w = [pltpu.roll(x, shift, 0) for x in v]
        v = [jnp.maximum(v[i], w[nslab - 1 - i]) for i in range(nslab)]
        j = nslab // 2
        while j >= 1:
            for i in range(nslab):
                if (i ^ j) > i:
                    _compare_exchange(v, i, i ^ j)
            j //= 2
        shift //= 2
    return v


def _slabs(x):
    return [x[SUBLANES * i:SUBLANES * (i + 1)] for i in range(x.shape[0] // SUBLANES)]


def _pair_counts(cand, sel, cmax):
    zsum = jnp.sum(jnp.where(sel, jnp.exp(cand - cmax), 0.0), axis=0, keepdims=True)
    self32 = jnp.where(sel, 1.0, 0.0)
    counts = [jnp.sum(self32[0:16], axis=0, keepdims=True)]
    counts += [jnp.sum(self32[16 + 8 * (a - 1):16 + 8 * a], axis=0, keepdims=True) for a in range(1, 8)]
    counts += [self32[72 + a - 8:73 + a - 8] for a in range(8, 16)]
    return zsum, counts


def _route_store(refs, ls, s1, s2, top1, top2, nb, in1, rank2, zsum):
    nb_ref, e1_ref, r2_ref, e2_ref = refs
    nb_ref[0, :, ls] = nb
    e1_ref[0, :, ls] = jnp.where(in1, jnp.exp(s1 - top1), 0.0)
    e2_ref[0, :, ls] = jnp.where(rank2 < float(P_TOPK), jnp.exp(s2 - top2) * (0.5 / zsum), 0.0)
    r2_ref[0, :, ls] = pltpu.bitcast(rank2.astype(BF16), jnp.uint32)


def _route_strip_distinct(refs, ls, s1, s2):
    t1 = _sorted_top16_slabs(s1)
    t2 = _sorted_top16_slabs(s2)
    v1 = jnp.concatenate([t[0:1] for t in t1], axis=0)
    v2 = jnp.concatenate([t[0:1] for t in t2], axis=0)
    cand = _candidates(v1, v2)
    _, _, cbelow, okc = _top16_distinct(cand, want_rank=False)
    zsum, counts = _pair_counts(cand, jnp.logical_not(cbelow), v1[0:1] + v2[0:1])
    nb_rows, rank_rows, n1, n2 = [], [], None, None
    for x1, x2 in zip(_slabs(s1), _slabs(s2)):
        nb = jnp.broadcast_to(counts[0], x1.shape)
        rank = jnp.zeros(x2.shape, F32)
        for a in range(1, P_TOPK):
            nb = jnp.where(x1 < t1[a - 1], counts[a], nb)
            rank = jnp.where(x2 < t2[a - 1], float(a), rank)
        out1, out2 = x1 < t1[P_TOPK - 1], x2 < t2[P_TOPK - 1]
        nb_rows.append(jnp.where(out1, 0.0, nb))
        rank_rows.append(jnp.where(out2, _UNRANKED, rank))
        c1, c2 = jnp.where(out1, 0.0, 1.0), jnp.where(out2, 0.0, 1.0)
        n1, n2 = (c1, c2) if n1 is None else (n1 + c1, n2 + c2)
    nb = jnp.concatenate(nb_rows, axis=0)
    rank2 = jnp.concatenate(rank_rows, axis=0)
    _route_store(refs, ls, s1, s2, v1[0:1], v2[0:1], nb, s1 >= v1[P_TOPK - 1:P_TOPK], rank2, zsum)
    strict = jnp.ones(t1[0].shape, F32)
    for a in range(1, P_TOPK):
        strict = jnp.where(jnp.logical_and(t1[a - 1] > t1[a], t2[a - 1] > t2[a]), strict, 0.0)
    ok1 = jnp.sum(n1, axis=0, keepdims=True) == float(P_TOPK)
    ok2 = jnp.sum(n2, axis=0, keepdims=True) == float(P_TOPK)
    ok = jnp.logical_and(jnp.logical_and(ok1, ok2), okc)
    return jnp.logical_and(ok, strict[0:1] > 0.5)


def _route_strip_exact(refs, ls, s1, s2):
    v1, rank1 = _top16_exact(s1)
    v2, rank2 = _top16_exact(s2)
    cand = _candidates(v1, v2)
    _, crank = _top16_exact(cand)
    zsum, counts = _pair_counts(cand, crank < float(P_TOPK), v1[0:1] + v2[0:1])
    nb = jnp.zeros(s1.shape, F32)
    for a in range(P_TOPK):
        nb = jnp.where(rank1 == float(a), counts[a], nb)
    _route_store(refs, ls, s1, s2, v1[0:1], v2[0:1], nb, rank1 < float(P_TOPK), rank2, zsum)


def _peer_route_kernel(xn_ref, wq_ref, k1_ref, k2_ref, nb_ref, e1_ref, r2_ref, e2_ref, s_ref):
    tb = xn_ref.shape[0]
    refs = (nb_ref, e1_ref, r2_ref, e2_ref)
    qt = lax.dot_general(wq_ref[...], xn_ref[...], NT_DIMS, preferred_element_type=F32)
    half = qt.shape[0] // 2
    s_ref[0] = jnp.dot(k1_ref[...], qt[:half].astype(BF16), preferred_element_type=F32)
    s_ref[1] = jnp.dot(k2_ref[...], qt[half:].astype(BF16), preferred_element_type=F32)
    strips = [slice(LANE * st, LANE * (st + 1)) for st in range(tb // LANE)]
    bad = []
    for ls in strips:
        ok = _route_strip_distinct(refs, ls, s_ref[0, :, ls], s_ref[1, :, ls])
        bad.append(jnp.max(jnp.where(ok, 0.0, 1.0)) > 0.5)

    for ls, bad_strip in zip(strips, bad):
        @pl.when(bad_strip)
        def _(ls=ls):
            _route_strip_exact(refs, ls, s_ref[0, :, ls], s_ref[1, :, ls])


def peer_route(xn, wq_t, k1, k2, tb):
    n, d = xn.shape
    hw = wq_t.shape[0] // P_HEADS
    ospec = pl.BlockSpec((1, P_NKEYS, tb), lambda i, h: (h, 0, i))
    pspec = pl.BlockSpec((1, P_NKEYS // 2, tb), lambda i, h: (h, 0, i))
    shape = lambda dt: jax.ShapeDtypeStruct((P_HEADS, P_NKEYS, n), dt)
    packed = jax.ShapeDtypeStruct((P_HEADS, P_NKEYS // 2, n), jnp.uint32)
    return pl.pallas_call(
        _peer_route_kernel,
        grid=(n // tb, P_HEADS),
        in_specs=[pl.BlockSpec((tb, d), lambda i, h: (i, 0)),
                  pl.BlockSpec((hw, d), lambda i, h: (h, 0)),
                  pl.BlockSpec((P_NKEYS, hw // 2), lambda i, h: (0, 0)),
                  pl.BlockSpec((P_NKEYS, hw // 2), lambda i, h: (0, 0))],
        out_specs=[ospec, ospec, pspec, ospec],
        out_shape=[shape(F32), shape(F32), packed, shape(F32)],
        scratch_shapes=[pltpu.VMEM((2, P_NKEYS, tb), F32)],
        compiler_params=_params("arbitrary", "arbitrary"),
        name="peer_route",
    )(xn, wq_t, k1, k2)


MXU_DIM = 256


BF16_ROWS = 16
C_GROUP = 1


def _peer_dense_kernel(xn_ref, u_ref, vt_ref, nb_ref, e1_ref, r2_ref, e2_ref, h_ref, y_ref,
                       acc_ref, hid_ref, act_ref, e2b_ref):
    j = pl.program_id(1)
    tb = xn_ref.shape[0]
    eb = u_ref.shape[0]
    n1 = eb // P_NKEYS
    nrg = P_NKEYS // BF16_ROWS
    half = BF16_ROWS // 2

    @pl.when(j == 0)
    def _():
        acc_ref[...] = jnp.zeros_like(acc_ref)
        for h in range(P_HEADS):
            e2b_ref[h] = e2_ref[h].astype(BF16)

    hid_ref[...] = lax.dot_general(u_ref[...], xn_ref[...], NT_DIMS, preferred_element_type=F32)
    zero = jnp.zeros((), BF16)

    def row_vreg(ref, c, h, ls):
        row = ref[pl.ds(h * P_NKEYS + j * n1 + c, 1), :]
        return jnp.broadcast_to(row[:, ls], (BF16_ROWS, LANE)).astype(BF16)

    for st in range(tb // LANE):
        ls = slice(LANE * st, LANE * (st + 1))
        for c0 in range(0, n1, C_GROUP):
            nbv = [[row_vreg(nb_ref, c0 + k, h, ls) for h in range(P_HEADS)] for k in range(C_GROUP)]
            e1v = [[row_vreg(e1_ref, c0 + k, h, ls) for h in range(P_HEADS)] for k in range(C_GROUP)]
            for rg in range(nrg):
                ws = slice(half * rg, half * (rg + 1))
                gs = [jnp.zeros((BF16_ROWS, LANE), BF16) for _ in range(C_GROUP)]
                for h in range(P_HEADS):
                    r2 = pltpu.bitcast(r2_ref[h, ws, ls], BF16)
                    e2 = e2b_ref[h, BF16_ROWS * rg:BF16_ROWS * (rg + 1), ls]
                    for k in range(C_GROUP):
                        gs[k] = gs[k] + jnp.where(r2 < nbv[k][h], e2, zero) * e1v[k][h]
                for k in range(C_GROUP):
                    rs = slice(P_NKEYS * (c0 + k) + BF16_ROWS * rg, P_NKEYS * (c0 + k) + BF16_ROWS * (rg + 1))
                    hh = hid_ref[rs, ls]
                    gelu2 = hh * (1.0 + lax.erf(hh * math.sqrt(0.5)))
                    act_ref[rs, ls] = gelu2.astype(BF16) * gs[k]
    acc_ref[...] += jnp.dot(vt_ref[...], act_ref[...], preferred_element_type=F32)

    @pl.when(j == pl.num_programs(1) - 1)
    def _():
        y_ref[...] = h_ref[...] + acc_ref[...].T


def peer_dense(xn, u, vt, nb, e1, r2, e2, h, tb, eb):
    n, d = xn.shape
    ne = u.shape[0]
    rspec = pl.BlockSpec((P_HEADS * P_NKEYS, tb), lambda i, j: (0, i))
    pspec = pl.BlockSpec((P_HEADS, P_NKEYS // 2, tb), lambda i, j: (0, 0, i))
    nb = nb.reshape(P_HEADS * P_NKEYS, n)
    e1 = e1.reshape(P_HEADS * P_NKEYS, n)
    return pl.pallas_call(
        _peer_dense_kernel,
        grid=(n // tb, ne // eb),
        in_specs=[pl.BlockSpec((tb, d), lambda i, j: (i, 0)),
                  pl.BlockSpec((eb, d), lambda i, j: (j, 0)),
                  pl.BlockSpec((d, eb), lambda i, j: (0, j)),
                  rspec, rspec, pspec, pl.BlockSpec((P_HEADS, P_NKEYS, tb), lambda i, j: (0, 0, i)),
                  pl.BlockSpec((tb, d), lambda i, j: (i, 0))],
        out_specs=pl.BlockSpec((tb, d), lambda i, j: (i, 0)),
        out_shape=jax.ShapeDtypeStruct((n, d), F32),
        scratch_shapes=[pltpu.VMEM((d, tb), F32), pltpu.VMEM((eb, tb), F32), pltpu.VMEM((eb, tb), BF16),
                        pltpu.VMEM((P_HEADS, P_NKEYS, tb), BF16)],
        compiler_params=_params("arbitrary", "arbitrary"),
        name="peer_dense",
    )(xn, u, vt, nb, e1, r2, e2, h)


TOKEN_TILE = 512
PEER_EXPERT_TILE = 16 * P_NKEYS
SAMPLE_BATCH_TILE = 8


def _permute_w_in(w_in):
    d = w_in.shape[0]
    seg = lambda a, b: w_in[:, a:b]
    lr = jnp.pad(seg(1792, 1808), ((0, 0), (0, LANE - B_RANK)))
    cols = [seg(2832, 5904), seg(0, 512), seg(1280, 1792), seg(1808, 2320), seg(2320, 2832), seg(768, 1024),
            seg(1024, 1280), seg(512, 640), seg(640, 768), lr, jnp.zeros((d, IN_COLS - COL_LR - LANE), w_in.dtype)]
    return jnp.concatenate(cols, axis=1).astype(BF16)


def kernel(x_prompt, x_sample, mem_prompt, cache_win_k, cache_win_v, state_gla, cache_mem_k, cache_mem_v, norm1_g, w_in, b_gates, a_qnorm_g, a_knorm_g, a_sinks, rel_bias, b_gate_w2, b_gate_b, b_gn_g, c_qnorm_g, c_knorm_g, mem_norm_g, w_mem_kv, w_br_a, w_br_b, w_br_c, w_out, norm2_g, peer_wq, peer_k1, peer_k2, peer_u, peer_v):
    depth = w_in.shape[0]
    assert depth == 1, "one layer"
    l = 0
    bp, tp, d = x_prompt.shape
    bs, ts, _ = x_sample.shape
    cw = C_HEADS * C_HD
    kvw = A_KV_HEADS * A_HD

    w_all = _permute_w_in(w_in[l])
    w2p = jnp.pad(b_gate_w2[l], ((0, LANE - B_RANK), (0, 0))).astype(BF16)
    wa, wb, wc, wo = (w.astype(BF16) for w in (w_br_a[l], w_br_b[l], w_br_c[l], w_out[l]))
    wq_t = peer_wq[l].T.astype(BF16)
    k1, k2 = peer_k1[l].astype(BF16), peer_k2[l].astype(BF16)
    u = peer_u[l].astype(BF16)
    vt = peer_v[l].T.astype(BF16)

    def channel_mix(oa, ob, oc, proj, x2):
        tm = min(TOKEN_TILE, x2.shape[0])
        h, xn2 = merge(oa, ob, oc, proj, b_gates[l], x2, wa, wb, wc, wo, norm2_g[l], tm)
        nb, e1, r2, e2 = peer_route(xn2, wq_t, k1, k2, tm)
        return peer_dense(xn2, u, vt, nb, e1, r2, e2, h, tm, PEER_EXPERT_TILE)

    xp = x_prompt.reshape(bp * tp, d)
    proj_p = norm_matmul(xp, norm1_g[l], w_all, min(TOKEN_TILE, bp * tp), IN_COLS // 2)
    memkv = norm_matmul(mem_prompt.reshape(bp * N_MEM, d), mem_norm_g[l], w_mem_kv[l].astype(BF16), min(TOKEN_TILE, bp * N_MEM), 2 * cw)
    oa_p, wk_p, wv_p = swa_prompt(proj_p, bp, tp, rel_bias, a_qnorm_g[l], a_knorm_g[l], a_sinks[l])
    nchunk = tp // B_CHUNK
    ob_p, st_p = gla(proj_p.reshape(bp, nchunk, B_CHUNK, IN_COLS), jnp.zeros((bp, B_HEADS, B_DV, B_DK), F32),
                     bp, w2p, b_gate_b[l], b_gn_g[l])
    memkv3 = memkv.reshape(bp, N_MEM, 2 * cw)
    oc_p, mk_p = xattn(proj_p.reshape(bp, tp, IN_COLS), memkv3, memkv3, 1, c_qnorm_g[l], c_knorm_g[l],
                       1, min(TOKEN_TILE, tp), True)
    y_p = channel_mix(oa_p, ob_p.reshape(bp * tp, -1), oc_p.reshape(bp * tp, cw), proj_p, xp)

    xs = x_sample.reshape(bs * ts, d)
    proj_s = norm_matmul(xs, norm1_g[l], w_all, min(TOKEN_TILE, bs * ts), IN_COLS // 2)
    proj_s3 = proj_s.reshape(bs, ts, IN_COLS)
    ck = cache_win_k[l].reshape(bs, -1, kvw)
    cv = cache_win_v[l].reshape(bs, -1, kvw)
    oa_s, knew = swa_sample(proj_s3, ck, cv, rel_bias, a_qnorm_g[l], a_knorm_g[l], a_sinks[l], SAMPLE_BATCH_TILE)
    vnew = proj_s3[:, :, COL_AV:COL_AV + kvw]
    ob_s, st_s = gla(proj_s3.reshape(bs, 1, ts, IN_COLS), jnp.swapaxes(state_gla[l], -1, -2),
                     math.gcd(bs, SAMPLE_BATCH_TILE), w2p, b_gate_b[l], b_gn_g[l])
    oc_s, = xattn(proj_s3, cache_mem_k[l].reshape(bs, N_MEM * C_HEADS, C_HD),
                  cache_mem_v[l].reshape(bs, N_MEM * C_HEADS, C_HD), 0,
                  c_qnorm_g[l], c_knorm_g[l], SAMPLE_BATCH_TILE, ts, False, token_major=True)
    y_s = channel_mix(oa_s.reshape(bs * ts, -1), ob_s.reshape(bs * ts, -1), oc_s.reshape(bs * ts, cw), proj_s, xs)

    wc_len = ck.shape[1]
    new_wk_s = jnp.concatenate([ck, knew], axis=1)[:, -wc_len:]
    new_wv_s = jnp.concatenate([cv, vnew], axis=1)[:, -wc_len:]
    return (y_p.reshape(bp, tp, d), y_s.reshape(bs, ts, d),
            wk_p.reshape(1, bp, WINDOW, A_KV_HEADS, A_HD), wv_p.reshape(1, bp, WINDOW, A_KV_HEADS, A_HD),
            jnp.swapaxes(st_p, -1, -2)[None],
            mk_p.reshape(1, bp, N_MEM, C_HEADS, C_HD), memkv3[:, :, cw:].reshape(1, bp, N_MEM, C_HEADS, C_HD),
            new_wk_s.reshape(1, bs, wc_len, A_KV_HEADS, A_HD), new_wv_s.reshape(1, bs, wc_len, A_KV_HEADS, A_HD),
            jnp.swapaxes(st_s, -1, -2)[None])
```

```python
import functools
import math

import numpy as np
import jax
import jax.numpy as jnp
from jax import lax
from jax.experimental import pallas as pl
from jax.experimental.pallas import tpu as pltpu

F32 = jnp.float32
BF16 = jnp.bfloat16
I32 = jnp.int32

EPS = 1e-6
D_MODEL = 1024
N_MEM = 256
A_HEADS, A_KV_HEADS, A_GROUP, A_HD = 8, 2, 4, 64
WINDOW = 128
N_BUCKETS, MAX_DISTANCE = 32, 128
B_HEADS, B_DK, B_DV, B_RANK, B_TAU = 4, 64, 128, 16, 16.0
B_CHUNK, B_SUB = 64, 16
C_HEADS, C_HD = 4, 128
P_HEADS, P_NKEYS, P_TOPK = 8, 128, 16
P_EXPERTS = P_NKEYS * P_NKEYS

LANE = 128
COL_GL, COL_AQ, COL_BV, COL_BR, COL_CQ, COL_BQ, COL_BK, COL_AK, COL_AV, COL_LR = (
    0, 3072, 3584, 4096, 4608, 5120, 5376, 5632, 5760, 5888)
IN_COLS = 6144
VMEM_LIMIT = 56 * 1024 * 1024

NT_DIMS = (((1,), (1,)), ((), ()))
TN_DIMS = (((0,), (0,)), ((), ()))


def _rms(x, g):
    ms = jnp.mean(x * x, axis=-1, keepdims=True)
    return x * lax.rsqrt(ms + EPS) * g


def _params(*sem, flags=None):
    return pltpu.CompilerParams(dimension_semantics=sem, vmem_limit_bytes=VMEM_LIMIT, flags=flags)


def _norm_matmul_kernel(x_ref, g_ref, w_ref, o_ref):
    xn = _rms(x_ref[...], g_ref[...]).astype(BF16)
    o_ref[...] = jnp.dot(xn, w_ref[...], preferred_element_type=F32)


def norm_matmul(x, g, w, tm, tn):
    n, d = x.shape
    m = w.shape[1]
    return pl.pallas_call(
        _norm_matmul_kernel,
        grid=(m // tn, n // tm),
        in_specs=[pl.BlockSpec((tm, d), lambda j, i: (i, 0)),
                  pl.BlockSpec((1, d), lambda j, i: (0, 0)),
                  pl.BlockSpec((d, tn), lambda j, i: (0, j))],
        out_specs=pl.BlockSpec((tm, tn), lambda j, i: (i, j)),
        out_shape=jax.ShapeDtypeStruct((n, m), F32),
        compiler_params=_params("arbitrary", "arbitrary"),
        name="norm_matmul",
    )(x, g.reshape(1, d), w)


def _t5_bucket_np(dist):
    n = np.maximum(dist, 0)
    max_exact = N_BUCKETS // 2
    nf = np.maximum(n, 1).astype(np.float32)
    large = max_exact + (np.log(nf / np.float32(max_exact)) / np.float32(math.log(MAX_DISTANCE / max_exact))
                         * np.float32(N_BUCKETS - max_exact)).astype(np.int32)
    large = np.minimum(large, N_BUCKETS - 1)
    return np.where(n < max_exact, n, large).astype(np.int32)


def _bias_from_buckets(bucket, rb_ref, head):
    tab = jnp.zeros(bucket.shape, F32)
    for bk in range(N_BUCKETS):
        tab = jnp.where(bucket == bk, rb_ref[bk, head], tab)
    return tab


def _swa_prompt_kernel(rb_ref, q_ref, kc_ref, kp_ref, vc_ref, vp_ref, gq_ref, gk_ref, snk_ref, bucket_ref,
                       o_ref, wk_ref, wv_ref, bias_ref):
    blk = WINDOW
    first = jnp.logical_and(pl.program_id(0) == 0, pl.program_id(1) == 0)

    @pl.when(first)
    def _():
        bucket = bucket_ref[...]
        qi = lax.broadcasted_iota(I32, (blk, 2 * blk), 0)
        kj = lax.broadcasted_iota(I32, (blk, 2 * blk), 1)
        dist = qi + blk - kj
        band = jnp.logical_and(dist >= 0, dist < WINDOW)
        for h in range(A_HEADS):
            bias_ref[h] = jnp.where(band, _bias_from_buckets(bucket, rb_ref, h), -jnp.inf)

    i = pl.program_id(1)
    nsub = q_ref.shape[0] // blk
    gq, gk = gq_ref[...], gk_ref[...]
    col = lax.broadcasted_iota(I32, (A_GROUP * blk, 2 * blk), 1)
    keep_first = jnp.logical_or(col >= blk, i > 0)
    scale = A_HD ** -0.5
    ones = jnp.ones((2 * blk, A_HD), BF16)
    kn = [[_rms(kp_ref[:, A_HD * g:A_HD * (g + 1)], gk)] +
          [_rms(kc_ref[blk * sb:blk * (sb + 1), A_HD * g:A_HD * (g + 1)], gk) for sb in range(nsub)]
          for g in range(A_KV_HEADS)]
    vv = [[vp_ref[:, A_HD * g:A_HD * (g + 1)]] +
          [vc_ref[blk * sb:blk * (sb + 1), A_HD * g:A_HD * (g + 1)] for sb in range(nsub)]
          for g in range(A_KV_HEADS)]
    for sb in range(nsub):
        q = q_ref[blk * sb:blk * (sb + 1), :]
        outs = [None] * A_HEADS
        for g in range(A_KV_HEADS):
            kcat = jnp.concatenate([kn[g][sb], kn[g][sb + 1]], axis=0).astype(BF16)
            vext = jnp.concatenate([jnp.concatenate([vv[g][sb], vv[g][sb + 1]], axis=0).astype(BF16), ones], axis=1)
            qg = jnp.concatenate(
                [_rms(q[:, A_HD * (A_GROUP * g + j):A_HD * (A_GROUP * g + j + 1)], gq) for j in range(A_GROUP)],
                axis=0)
            s = lax.dot_general(qg.astype(BF16), kcat, NT_DIMS, preferred_element_type=F32) * scale
            s = s + bias_ref[A_GROUP * g:A_GROUP * (g + 1)].reshape(A_GROUP * blk, 2 * blk)
            if sb == 0:
                s = jnp.where(keep_first, s, -jnp.inf)
            snk = snk_ref[g]
            m = jnp.maximum(jnp.max(s, axis=-1, keepdims=True), snk)
            p = jnp.exp(s - m)
            oext = jnp.dot(p.astype(BF16), vext, preferred_element_type=F32)
            den = oext[:, A_HD:A_HD + 1] + jnp.exp(snk - m)
            o = oext[:, :A_HD] / den
            for j in range(A_GROUP):
                outs[A_GROUP * g + j] = o[blk * j:blk * (j + 1)]
        o_ref[blk * sb:blk * (sb + 1), :] = jnp.concatenate(outs, axis=-1)
    wk_ref[0] = jnp.concatenate([kn[g][nsub] for g in range(A_KV_HEADS)], axis=-1)
    wv_ref[0] = vc_ref[blk * (nsub - 1):blk * nsub, :]


SWA_QUERY_BLOCKS = 4


def swa_prompt(proj, batch, seq, rel_bias, gq, gk, sinks):
    blk = WINDOW
    nsub = math.gcd(SWA_QUERY_BLOCKS, seq // blk)
    tq = nsub * blk
    nb = seq // tq
    qi = np.arange(blk)[:, None]
    kj = np.arange(2 * blk)[None, :]
    bucket = jnp.asarray(_t5_bucket_np(qi + blk - kj))
    snk = jnp.broadcast_to(sinks.astype(F32).reshape(A_KV_HEADS, A_GROUP, 1, 1),
                           (A_KV_HEADS, A_GROUP, blk, 1)).reshape(A_KV_HEADS, A_GROUP * blk, 1)
    cur = lambda c: (lambda b, i: (b * nb + i, c))
    prev = lambda c: (lambda b, i: (jnp.maximum((b * nb + i) * nsub - 1, b * nb * nsub), c))
    kvw = A_KV_HEADS * A_HD
    return pl.pallas_call(
        _swa_prompt_kernel,
        grid=(batch, nb),
        in_specs=[pl.BlockSpec(memory_space=pltpu.SMEM),
                  pl.BlockSpec((tq, A_HEADS * A_HD), cur(COL_AQ // (A_HEADS * A_HD))),
                  pl.BlockSpec((tq, kvw), cur(COL_AK // kvw)),
                  pl.BlockSpec((blk, kvw), prev(COL_AK // kvw)),
                  pl.BlockSpec((tq, kvw), cur(COL_AV // kvw)),
                  pl.BlockSpec((blk, kvw), prev(COL_AV // kvw)),
                  pl.BlockSpec((1, A_HD), lambda b, i: (0, 0)),
                  pl.BlockSpec((1, A_HD), lambda b, i: (0, 0)),
                  pl.BlockSpec((A_KV_HEADS, A_GROUP * blk, 1), lambda b, i: (0, 0, 0)),
                  pl.BlockSpec((blk, 2 * blk), lambda b, i: (0, 0))],
        out_specs=[pl.BlockSpec((tq, A_HEADS * A_HD), lambda b, i: (b * nb + i, 0)),
                   pl.BlockSpec((1, blk, kvw), lambda b, i: (b, 0, 0)),
                   pl.BlockSpec((1, blk, kvw), lambda b, i: (b, 0, 0))],
        out_shape=[jax.ShapeDtypeStruct((batch * seq, A_HEADS * A_HD), F32),
                   jax.ShapeDtypeStruct((batch, blk, kvw), F32),
                   jax.ShapeDtypeStruct((batch, blk, kvw), F32)],
        scratch_shapes=[pltpu.VMEM((A_HEADS, blk, 2 * blk), F32)],
        compiler_params=_params("arbitrary", "arbitrary"),
        name="swa_prompt",
    )(rel_bias.astype(F32), proj, proj, proj, proj, proj, gq.reshape(1, A_HD), gk.reshape(1, A_HD), snk, bucket)


def _swa_sample_kernel(rb_ref, q_ref, kn_ref, vn_ref, ck_ref, cv_ref, gq_ref, gk_ref, snk_ref, bkt_ref, valid_ref,
                       o_ref, knew_ref, bias_ref, *, t_new):
    bb = q_ref.shape[0]
    wc = ck_ref.shape[1]
    nrow = A_GROUP * t_new
    pad = bias_ref.shape[2] - wc - t_new

    @pl.when(pl.program_id(0) == 0)
    def _():
        row = lax.broadcasted_iota(I32, bkt_ref.shape, 0)
        for g in range(A_KV_HEADS):
            tab = jnp.zeros(bkt_ref.shape, F32)
            for j in range(A_GROUP):
                in_head = jnp.logical_and(row >= j * t_new, row < (j + 1) * t_new)
                tab = jnp.where(in_head, _bias_from_buckets(bkt_ref[...], rb_ref, A_GROUP * g + j), tab)
            bias_ref[g] = jnp.where(valid_ref[...] > 0.5, tab, -jnp.inf)

    gq, gk = gq_ref[...], gk_ref[...]
    scale = A_HD ** -0.5
    zrows = jnp.zeros((pad, A_HD), F32)
    for i in range(bb):
        q = q_ref[i]
        outs = [None] * A_HEADS
        kn_all = []
        for g in range(A_KV_HEADS):
            ks = slice(A_HD * g, A_HD * (g + 1))
            knn = _rms(kn_ref[i, :, ks], gk)
            kn_all.append(knn)
            kfull = jnp.concatenate([ck_ref[i, :, ks], knn, zrows], axis=0).astype(BF16)
            vfull = jnp.concatenate([cv_ref[i, :, ks], vn_ref[i, :, ks], zrows], axis=0).astype(BF16)
            qg = jnp.concatenate(
                [_rms(q[:, A_HD * (A_GROUP * g + j):A_HD * (A_GROUP * g + j + 1)], gq) for j in range(A_GROUP)],
                axis=0)
            s = lax.dot_general(qg.astype(BF16), kfull, NT_DIMS, preferred_element_type=F32) * scale + bias_ref[g]
            snk = snk_ref[g]
            m = jnp.maximum(jnp.max(s, axis=-1, keepdims=True), snk)
            p = jnp.exp(s - m)
            den = jnp.sum(p, axis=-1, keepdims=True) + jnp.exp(snk - m)
            o = jnp.dot(p.astype(BF16), vfull, preferred_element_type=F32) / den
            for j in range(A_GROUP):
                outs[A_GROUP * g + j] = o[t_new * j:t_new * (j + 1)]
        o_ref[i] = jnp.concatenate(outs, axis=-1)
        knew_ref[i] = jnp.concatenate(kn_all, axis=-1)


def swa_sample(proj3, cache_k, cache_v, rel_bias, gq, gk, sinks, bb):
    batch, t_new, _ = proj3.shape
    wc = cache_k.shape[1]
    kvw = A_KV_HEADS * A_HD
    nrow = A_GROUP * t_new
    ncol = wc + 2 * SUBLANES * (-(-t_new // (2 * SUBLANES)))
    t = (np.arange(nrow) % t_new)[:, None]
    c = np.arange(ncol)[None, :]
    dist = np.where(c < wc, wc + t - c, t - (c - wc))
    valid = (dist >= 0) & (dist < WINDOW) & (c < wc + t_new)
    bkt = jnp.asarray(_t5_bucket_np(dist))
    snk = jnp.broadcast_to(sinks.astype(F32).reshape(A_KV_HEADS, A_GROUP, 1, 1),
                           (A_KV_HEADS, A_GROUP, t_new, 1)).reshape(A_KV_HEADS, nrow, 1)
    whole = lambda *shape: pl.BlockSpec(shape, lambda i: (0,) * len(shape))
    return pl.pallas_call(
        functools.partial(_swa_sample_kernel, t_new=t_new),
        grid=(batch // bb,),
        in_specs=[pl.BlockSpec(memory_space=pltpu.SMEM),
                  pl.BlockSpec((bb, t_new, A_HEADS * A_HD), lambda i: (i, 0, COL_AQ // (A_HEADS * A_HD))),
                  pl.BlockSpec((bb, t_new, kvw), lambda i: (i, 0, COL_AK // kvw)),
                  pl.BlockSpec((bb, t_new, kvw), lambda i: (i, 0, COL_AV // kvw)),
                  pl.BlockSpec((bb, wc, kvw), lambda i: (i, 0, 0)),
                  pl.BlockSpec((bb, wc, kvw), lambda i: (i, 0, 0)),
                  whole(1, A_HD), whole(1, A_HD), whole(A_KV_HEADS, nrow, 1), whole(nrow, ncol), whole(nrow, ncol)],
        out_specs=[pl.BlockSpec((bb, t_new, A_HEADS * A_HD), lambda i: (i, 0, 0)),
                   pl.BlockSpec((bb, t_new, kvw), lambda i: (i, 0, 0))],
        out_shape=[jax.ShapeDtypeStruct((batch, t_new, A_HEADS * A_HD), F32),
                   jax.ShapeDtypeStruct((batch, t_new, kvw), F32)],
        scratch_shapes=[pltpu.VMEM((A_KV_HEADS, nrow, ncol), F32)],
        compiler_params=_params("arbitrary"),
        name="swa_sample",
    )(rel_bias.astype(F32), proj3, proj3, proj3, cache_k, cache_v,
      gq.reshape(1, A_HD), gk.reshape(1, A_HD), snk, bkt, jnp.asarray(valid.astype(np.float32)))


def _gla_kernel(q_ref, k_ref, v_ref, lr_ref, r_ref, s0_ref, w2_ref, gb_ref, gn_ref, o_ref, sout_ref, st_ref,
                *, chunk, sub):
    c = pl.program_id(1)
    bb = q_ref.shape[0]

    @pl.when(c == 0)
    def _():
        st_ref[...] = s0_ref[...]

    gn = gn_ref[...]
    row = lax.broadcasted_iota(I32, (chunk, chunk), 0)
    col = lax.broadcasted_iota(I32, (chunk, chunk), 1)
    nsub = chunk // sub
    for i in range(bb):
        z = jnp.dot(lr_ref[i, 0].astype(BF16), w2_ref[...], preferred_element_type=F32) + gb_ref[...]
        b = (jnp.minimum(z, 0.0) - jnp.log1p(jnp.exp(-jnp.abs(z)))) / B_TAU
        row_w = lax.broadcasted_iota(I32, b.shape, 0)
        shift = 1
        while shift < chunk:
            b = b + jnp.where(row_w >= shift, pltpu.roll(b, shift, 0), 0.0)
            shift *= 2
        q = q_ref[i, 0] * (B_DK ** -0.5)
        k, v, r = k_ref[i, 0], v_ref[i, 0], r_ref[i, 0]
        outs = []
        for h in range(B_HEADS):
            ks = slice(B_DK * h, B_DK * (h + 1))
            vs = slice(B_DV * h, B_DV * (h + 1))
            qh, kh, vh, bh = q[:, ks], k[:, ks], v[:, vs], b[:, ks]
            st = st_ref[i, h]
            o = lax.dot_general((qh * jnp.exp(bh)).astype(BF16), st.astype(BF16), NT_DIMS,
                                preferred_element_type=F32)
            refs = [bh[sub * (j + 1) - 1:sub * (j + 1)] for j in range(nsub)]
            rfull = jnp.concatenate([jnp.broadcast_to(rj, (sub, B_DK)) for rj in refs], axis=0)
            kt = (kh * jnp.exp(rfull - bh)).astype(BF16)
            qcat = jnp.concatenate([qh * jnp.exp(jnp.minimum(bh - rj, 80.0)) for rj in refs], axis=0).astype(BF16)
            aa = lax.dot_general(qcat, kt, NT_DIMS, preferred_element_type=F32)
            att = jnp.zeros((chunk, chunk), F32)
            for j in range(nsub):
                in_sub = jnp.logical_and(col >= sub * j, col < sub * (j + 1))
                att = jnp.where(in_sub, aa[chunk * j:chunk * (j + 1)], att)
            att = jnp.where(col <= row, att, 0.0)
            o = o + jnp.dot(att.astype(BF16), vh.astype(BF16), preferred_element_type=F32)
            bl = bh[chunk - 1:chunk]
            kd = (kh * jnp.exp(bl - bh)).astype(BF16)
            st_ref[i, h] = st * jnp.exp(bl) + lax.dot_general(vh.astype(BF16), kd, TN_DIMS,
                                                              preferred_element_type=F32)
            rh = r[:, vs]
            outs.append(_rms(o, gn) * (rh * jax.nn.sigmoid(rh)))
        o_ref[i, 0] = jnp.concatenate(outs, axis=-1)
    sout_ref[...] = st_ref[...]


def gla(proj4, state_t, bb, w2p, gbias, gn):
    batch, nchunk, chunk, _ = proj4.shape
    sub = min(B_SUB, chunk)
    kw, vw = B_HEADS * B_DK, B_HEADS * B_DV
    blk = lambda w, col: pl.BlockSpec((bb, 1, chunk, w), lambda b, c: (b, c, 0, col // w))
    whole = lambda *shape: pl.BlockSpec(shape, lambda b, c: (0,) * len(shape))
    st_spec = pl.BlockSpec((bb, B_HEADS, B_DV, B_DK), lambda b, c: (b, 0, 0, 0))
    return pl.pallas_call(
        functools.partial(_gla_kernel, chunk=chunk, sub=sub),
        grid=(batch // bb, nchunk),
        in_specs=[blk(kw, COL_BQ), blk(kw, COL_BK), blk(vw, COL_BV), blk(LANE, COL_LR), blk(vw, COL_BR), st_spec,
                  whole(LANE, kw), whole(1, kw), whole(1, B_DV)],
        out_specs=[pl.BlockSpec((bb, 1, chunk, vw), lambda b, c: (b, c, 0, 0)), st_spec],
        out_shape=[jax.ShapeDtypeStruct((batch, nchunk, chunk, vw), F32),
                   jax.ShapeDtypeStruct((batch, B_HEADS, B_DV, B_DK), F32)],
        scratch_shapes=[pltpu.VMEM((bb, B_HEADS, B_DV, B_DK), F32)],
        compiler_params=_params("arbitrary", "arbitrary"),
        name="gla",
    )(proj4, proj4, proj4, proj4, proj4, state_t, w2p, gbias.reshape(1, kw), gn.reshape(1, B_DV))


def _xattn_kernel(q_ref, k_ref, v_ref, gq_ref, gk_ref, o_ref, kn_ref):
    gq, gk = gq_ref[...], gk_ref[...]
    scale = C_HD ** -0.5
    q = q_ref[0]
    outs, kns = [], []
    for h in range(C_HEADS):
        hs = slice(C_HD * h, C_HD * (h + 1))
        kh = _rms(k_ref[0, :, hs], gk)
        kns.append(kh)
        qh = _rms(q[:, hs], gq)
        s = lax.dot_general(qh.astype(BF16), kh.astype(BF16), NT_DIMS, preferred_element_type=F32) * scale
        p = jnp.exp(s - jnp.max(s, axis=-1, keepdims=True))
        p = p / jnp.sum(p, axis=-1, keepdims=True)
        outs.append(jnp.dot(p.astype(BF16), v_ref[0, :, hs].astype(BF16), preferred_element_type=F32))
    o_ref[0] = jnp.concatenate(outs, axis=-1)
    kn_ref[0] = jnp.concatenate(kns, axis=-1)


def xattn(proj3, mem_kv, gq, gk, tq):
    batch, t, _ = proj3.shape
    cw = C_HEADS * C_HD
    whole = lambda *shape: pl.BlockSpec(shape, lambda b, i: (0,) * len(shape))
    return pl.pallas_call(
        _xattn_kernel,
        grid=(batch, t // tq),
        in_specs=[pl.BlockSpec((1, tq, cw), lambda b, i: (b, i, COL_CQ // cw)),
                  pl.BlockSpec((1, N_MEM, cw), lambda b, i: (b, 0, 0)),
                  pl.BlockSpec((1, N_MEM, cw), lambda b, i: (b, 0, 1)),
                  whole(1, C_HD), whole(1, C_HD)],
        out_specs=[pl.BlockSpec((1, tq, cw), lambda b, i: (b, i, 0)),
                   pl.BlockSpec((1, N_MEM, cw), lambda b, i: (b, 0, 0))],
        out_shape=[jax.ShapeDtypeStruct((batch, t, cw), F32), jax.ShapeDtypeStruct((batch, N_MEM, cw), F32)],
        compiler_params=_params("arbitrary", "arbitrary"),
        name="xattn",
    )(proj3, mem_kv, mem_kv, gq.reshape(1, C_HD), gk.reshape(1, C_HD))


def _xattn_decode_kernel(q_ref, k_ref, v_ref, gq_ref, o_ref, *, bb):
    tq = q_ref.shape[1]
    nrow, ncol = C_HEADS * tq, k_ref.shape[1]
    row = lax.broadcasted_iota(I32, (nrow, ncol), 0)
    col = lax.broadcasted_iota(I32, (nrow, ncol), 1)
    row_head = jnp.zeros((nrow, ncol), I32)
    for h in range(1, C_HEADS):
        row_head = row_head + jnp.where(row >= h * tq, 1, 0)
    same_head = (col & (C_HEADS - 1)) == row_head
    gq = gq_ref[...]
    scale = C_HD ** -0.5
    for i in range(bb):
        q = q_ref[i]
        qb = jnp.concatenate([_rms(q[:, C_HD * h:C_HD * (h + 1)], gq) for h in range(C_HEADS)], axis=0)
        s = lax.dot_general(qb.astype(BF16), k_ref[i].astype(BF16), NT_DIMS, preferred_element_type=F32) * scale
        s = jnp.where(same_head, s, -jnp.inf)
        p = jnp.exp(s - jnp.max(s, axis=-1, keepdims=True))
        o = jnp.dot(p.astype(BF16), v_ref[i].astype(BF16), preferred_element_type=F32)
        o = o / jnp.sum(p, axis=-1, keepdims=True)
        o_ref[i] = jnp.concatenate([o[tq * h:tq * (h + 1)] for h in range(C_HEADS)], axis=-1)


def xattn_decode(proj3, mem_k, mem_v, gq, bb):
    assert C_HEADS & (C_HEADS - 1) == 0
    batch, tq, _ = proj3.shape
    cw = C_HEADS * C_HD
    kspec = pl.BlockSpec((bb, mem_k.shape[1], C_HD), lambda b: (b, 0, 0))
    return pl.pallas_call(
        functools.partial(_xattn_decode_kernel, bb=bb),
        grid=(batch // bb,),
        in_specs=[pl.BlockSpec((bb, tq, cw), lambda b: (b, 0, COL_CQ // cw)), kspec, kspec,
                  pl.BlockSpec((1, C_HD), lambda b: (0, 0))],
        out_specs=pl.BlockSpec((bb, tq, cw), lambda b: (b, 0, 0)),
        out_shape=jax.ShapeDtypeStruct((batch, tq, cw), F32),
        compiler_params=_params("arbitrary"),
        name="xattn_decode",
    )(proj3, mem_k, mem_v, gq.reshape(1, C_HD))


def _merge_kernel(oa_ref, ob_ref, oc_ref, gl_ref, bg_ref, x_ref, wa_ref, wb_ref, wc_ref, wo_ref, g2_ref,
                  h_ref, xn_ref):
    d = D_MODEL
    gate = jax.nn.sigmoid(gl_ref[...] + bg_ref[...])
    pa = jnp.dot(oa_ref[...].astype(BF16), wa_ref[...], preferred_element_type=F32)
    pb = jnp.dot(ob_ref[...].astype(BF16), wb_ref[...], preferred_element_type=F32)
    pc = jnp.dot(oc_ref[...].astype(BF16), wc_ref[...], preferred_element_type=F32)
    merged = gate[:, :d] * pa + gate[:, d:2 * d] * pb + gate[:, 2 * d:] * pc
    h = x_ref[...] + jnp.dot(merged.astype(BF16), wo_ref[...], preferred_element_type=F32)
    h_ref[...] = h
    xn_ref[...] = _rms(h, g2_ref[...]).astype(BF16)


def merge(oa, ob, oc, proj, b_gates, x, wa, wb, wc, wo, g2, tm):
    n, d = x.shape
    bw = oa.shape[1]
    tok = lambda w: pl.BlockSpec((tm, w), lambda i: (i, 0))
    whole = lambda *shape: pl.BlockSpec(shape, lambda i: (0,) * len(shape))
    return pl.pallas_call(
        _merge_kernel,
        grid=(n // tm,),
        in_specs=[tok(bw), tok(bw), tok(bw), tok(3 * d), whole(1, 3 * d), tok(d),
                  whole(bw, d), whole(bw, d), whole(bw, d), whole(d, d), whole(1, d)],
        out_specs=[tok(d), tok(d)],
        out_shape=[jax.ShapeDtypeStruct((n, d), F32), jax.ShapeDtypeStruct((n, d), BF16)],
        compiler_params=_params("arbitrary"),
        name="merge",
    )(oa, ob, oc, proj, b_gates.reshape(1, 3 * d), x, wa, wb, wc, wo, g2.reshape(1, d))


_UNRANKED = float(P_NKEYS)


def _top16_exact(s):
    nrow = s.shape[0]
    row = lax.broadcasted_iota(I32, s.shape, 0)
    rank = jnp.full(s.shape, _UNRANKED, F32)
    vals = []
    for kk in range(P_TOPK):
        m = jnp.max(s, axis=0, keepdims=True)
        idx = jnp.min(jnp.where(s == m, row, nrow), axis=0, keepdims=True)
        hit = row == idx
        rank = jnp.where(hit, float(kk), rank)
        s = jnp.where(hit, -jnp.inf, s)
        vals.append(m)
    return jnp.concatenate(vals, axis=0), rank


def _top16_distinct(s, want_rank=True):
    rank = jnp.zeros(s.shape, F32)
    m = jnp.max(s, axis=0, keepdims=True)
    vals = [m]
    for kk in range(1, P_TOPK):
        below = s < m
        if want_rank:
            rank = jnp.where(below, float(kk), rank)
        m = jnp.max(jnp.where(below, s, -jnp.inf), axis=0, keepdims=True)
        vals.append(m)
    below = s < m
    rank = jnp.where(below, _UNRANKED, rank)
    count = jnp.sum(jnp.where(below, 0.0, 1.0), axis=0, keepdims=True)
    return jnp.concatenate(vals, axis=0), rank, below, count == float(P_TOPK)


_CAND_ROWS = 16 + 7 * 8 + 8


def _candidates(v1, v2):
    blocks = [v1[0:1] + v2[0:16]]
    blocks += [v1[a:a + 1] + v2[0:8] for a in range(1, 8)]
    blocks.append(v1[8:16] + v2[0:1])
    return jnp.concatenate(blocks, axis=0)


SUBLANES = 8


def _compare_exchange(v, i, l):
    v[i], v[l] = jnp.maximum(v[i], v[l]), jnp.minimum(v[i], v[l])


def _sorted_top16_slabs(s):
    nslab = s.shape[0] // SUBLANES
    assert nslab == P_TOPK
    v = [s[SUBLANES * i:SUBLANES * (i + 1)] for i in range(nslab)]
    k = 2
    while k <= nslab:
        j = k // 2
        while j >= 1:
            for i in range(nslab):
                l = i ^ j
                if l > i:
                    if (i & k) == 0:
                        _compare_exchange(v, i, l)
                    else:
                        _compare_exchange(v, l, i)
            j //= 2
        k *= 2
    shift = SUBLANES // 2
    while shift >= 1:
        w = [pltpu.roll(x, shift, 0) for x in v]
        v = [jnp.maximum(v[i], w[nslab - 1 - i]) for i in range(nslab)]
        j = nslab // 2
        while j >= 1:
            for i in range(nslab):
                if (i ^ j) > i:
                    _compare_exchange(v, i, i ^ j)
            j //= 2
        shift //= 2
    return v


def _slabs(x):
    return [x[SUBLANES * i:SUBLANES * (i + 1)] for i in range(x.shape[0] // SUBLANES)]


def _pair_counts(cand, sel, cmax):
    zsum = jnp.sum(jnp.where(sel, jnp.exp(cand - cmax), 0.0), axis=0, keepdims=True)
    self32 = jnp.where(sel, 1.0, 0.0)
    counts = [jnp.sum(self32[0:16], axis=0, keepdims=True)]
    counts += [jnp.sum(self32[16 + 8 * (a - 1):16 + 8 * a], axis=0, keepdims=True) for a in range(1, 8)]
    counts += [self32[72 + a - 8:73 + a - 8] for a in range(8, 16)]
    return zsum, counts


def _route_store(refs, ls, s1, s2, top1, top2, nb, in1, rank2, zsum):
    nb_ref, e1_ref, r2_ref, e2_ref = refs
    nb_ref[0, :, ls] = nb
    e1_ref[0, :, ls] = jnp.where(in1, jnp.exp(s1 - top1), 0.0)
    e2_ref[0, :, ls] = jnp.where(rank2 < float(P_TOPK), jnp.exp(s2 - top2) * (0.5 / zsum), 0.0)
    r2_ref[0, :, ls] = pltpu.bitcast(rank2.astype(BF16), jnp.uint32)


def _route_strip_distinct(refs, ls, s1, s2):
    t1 = _sorted_top16_slabs(s1)
    t2 = _sorted_top16_slabs(s2)
    v1 = jnp.concatenate([t[0:1] for t in t1], axis=0)
    v2 = jnp.concatenate([t[0:1] for t in t2], axis=0)
    cand = _candidates(v1, v2)
    _, _, cbelow, okc = _top16_distinct(cand, want_rank=False)
    zsum, counts = _pair_counts(cand, jnp.logical_not(cbelow), v1[0:1] + v2[0:1])
    nb_rows, rank_rows, n1, n2 = [], [], None, None
    for x1, x2 in zip(_slabs(s1), _slabs(s2)):
        nb = jnp.broadcast_to(counts[0], x1.shape)
        rank = jnp.zeros(x2.shape, F32)
        for a in range(1, P_TOPK):
            nb = jnp.where(x1 < t1[a - 1], counts[a], nb)
            rank = jnp.where(x2 < t2[a - 1], float(a), rank)
        out1, out2 = x1 < t1[P_TOPK - 1], x2 < t2[P_TOPK - 1]
        nb_rows.append(jnp.where(out1, 0.0, nb))
        rank_rows.append(jnp.where(out2, _UNRANKED, rank))
        c1, c2 = jnp.where(out1, 0.0, 1.0), jnp.where(out2, 0.0, 1.0)
        n1, n2 = (c1, c2) if n1 is None else (n1 + c1, n2 + c2)
    nb = jnp.concatenate(nb_rows, axis=0)
    rank2 = jnp.concatenate(rank_rows, axis=0)
    _route_store(refs, ls, s1, s2, v1[0:1], v2[0:1], nb, s1 >= v1[P_TOPK - 1:P_TOPK], rank2, zsum)
    strict = jnp.ones(t1[0].shape, F32)
    for a in range(1, P_TOPK):
        strict = jnp.where(jnp.logical_and(t1[a - 1] > t1[a], t2[a - 1] > t2[a]), strict, 0.0)
    ok1 = jnp.sum(n1, axis=0, keepdims=True) == float(P_TOPK)
    ok2 = jnp.sum(n2, axis=0, keepdims=True) == float(P_TOPK)
    ok = jnp.logical_and(jnp.logical_and(ok1, ok2), okc)
    return jnp.logical_and(ok, strict[0:1] > 0.5)


def _route_strip_exact(refs, ls, s1, s2):
    v1, rank1 = _top16_exact(s1)
    v2, rank2 = _top16_exact(s2)
    cand = _candidates(v1, v2)
    _, crank = _top16_exact(cand)
    zsum, counts = _pair_counts(cand, crank < float(P_TOPK), v1[0:1] + v2[0:1])
    nb = jnp.zeros(s1.shape, F32)
    for a in range(P_TOPK):
        nb = jnp.where(rank1 == float(a), counts[a], nb)
    _route_store(refs, ls, s1, s2, v1[0:1], v2[0:1], nb, rank1 < float(P_TOPK), rank2, zsum)


def _peer_route_kernel(xn_ref, wq_ref, k1_ref, k2_ref, nb_ref, e1_ref, r2_ref, e2_ref, s_ref):
    tb = xn_ref.shape[0]
    refs = (nb_ref, e1_ref, r2_ref, e2_ref)
    qt = lax.dot_general(wq_ref[...], xn_ref[...], NT_DIMS, preferred_element_type=F32)
    half = qt.shape[0] // 2
    s_ref[0] = jnp.dot(k1_ref[...], qt[:half].astype(BF16), preferred_element_type=F32)
    s_ref[1] = jnp.dot(k2_ref[...], qt[half:].astype(BF16), preferred_element_type=F32)
    strips = [slice(LANE * st, LANE * (st + 1)) for st in range(tb // LANE)]
    bad = []
    for ls in strips:
        ok = _route_strip_distinct(refs, ls, s_ref[0, :, ls], s_ref[1, :, ls])
        bad.append(jnp.max(jnp.where(ok, 0.0, 1.0)) > 0.5)

    for ls, bad_strip in zip(strips, bad):
        @pl.when(bad_strip)
        def _(ls=ls):
            _route_strip_exact(refs, ls, s_ref[0, :, ls], s_ref[1, :, ls])


def peer_route(xn, wq_t, k1, k2, tb):
    n, d = xn.shape
    hw = wq_t.shape[0] // P_HEADS
    ospec = pl.BlockSpec((1, P_NKEYS, tb), lambda i, h: (h, 0, i))
    pspec = pl.BlockSpec((1, P_NKEYS // 2, tb), lambda i, h: (h, 0, i))
    shape = lambda dt: jax.ShapeDtypeStruct((P_HEADS, P_NKEYS, n), dt)
    packed = jax.ShapeDtypeStruct((P_HEADS, P_NKEYS // 2, n), jnp.uint32)
    return pl.pallas_call(
        _peer_route_kernel,
        grid=(n // tb, P_HEADS),
        in_specs=[pl.BlockSpec((tb, d), lambda i, h: (i, 0)),
                  pl.BlockSpec((hw, d), lambda i, h: (h, 0)),
                  pl.BlockSpec((P_NKEYS, hw // 2), lambda i, h: (0, 0)),
                  pl.BlockSpec((P_NKEYS, hw // 2), lambda i, h: (0, 0))],
        out_specs=[ospec, ospec, pspec, ospec],
        out_shape=[shape(F32), shape(F32), packed, shape(F32)],
        scratch_shapes=[pltpu.VMEM((2, P_NKEYS, tb), F32)],
        compiler_params=_params("arbitrary", "arbitrary"),
        name="peer_route",
    )(xn, wq_t, k1, k2)


MXU_DIM = 256


BF16_ROWS = 16
C_GROUP = 1


def _peer_dense_kernel(xn_ref, u_ref, vt_ref, nb_ref, e1_ref, r2_ref, e2_ref, h_ref, y_ref,
                       acc_ref, hid_ref, act_ref, e2b_ref):
    j = pl.program_id(1)
    tb = xn_ref.shape[0]
    eb = u_ref.shape[0]
    n1 = eb // P_NKEYS
    nrg = P_NKEYS // BF16_ROWS
    half = BF16_ROWS // 2

    @pl.when(j == 0)
    def _():
        acc_ref[...] = jnp.zeros_like(acc_ref)
        for h in range(P_HEADS):
            e2b_ref[h] = e2_ref[h].astype(BF16)

    hid_ref[...] = lax.dot_general(u_ref[...], xn_ref[...], NT_DIMS, preferred_element_type=F32)
    zero = jnp.zeros((), BF16)

    def row_vreg(ref, c, h, ls):
        row = ref[pl.ds(h * P_NKEYS + j * n1 + c, 1), :]
        return jnp.broadcast_to(row[:, ls], (BF16_ROWS, LANE)).astype(BF16)

    for st in range(tb // LANE):
        ls = slice(LANE * st, LANE * (st + 1))
        for c0 in range(0, n1, C_GROUP):
            nbv = [[row_vreg(nb_ref, c0 + k, h, ls) for h in range(P_HEADS)] for k in range(C_GROUP)]
            e1v = [[row_vreg(e1_ref, c0 + k, h, ls) for h in range(P_HEADS)] for k in range(C_GROUP)]
            for rg in range(nrg):
                ws = slice(half * rg, half * (rg + 1))
                gs = [jnp.zeros((BF16_ROWS, LANE), BF16) for _ in range(C_GROUP)]
                for h in range(P_HEADS):
                    r2 = pltpu.bitcast(r2_ref[h, ws, ls], BF16)
                    e2 = e2b_ref[h, BF16_ROWS * rg:BF16_ROWS * (rg + 1), ls]
                    for k in range(C_GROUP):
                        gs[k] = gs[k] + jnp.where(r2 < nbv[k][h], e2, zero) * e1v[k][h]
                for k in range(C_GROUP):
                    rs = slice(P_NKEYS * (c0 + k) + BF16_ROWS * rg, P_NKEYS * (c0 + k) + BF16_ROWS * (rg + 1))
                    hh = hid_ref[rs, ls]
                    gelu2 = hh * (1.0 + lax.erf(hh * math.sqrt(0.5)))
                    act_ref[rs, ls] = gelu2.astype(BF16) * gs[k]
    acc_ref[...] += jnp.dot(vt_ref[...], act_ref[...], preferred_element_type=F32)

    @pl.when(j == pl.num_programs(1) - 1)
    def _():
        y_ref[...] = h_ref[...] + acc_ref[...].T


def peer_dense(xn, u, vt, nb, e1, r2, e2, h, tb, eb):
    n, d = xn.shape
    ne = u.shape[0]
    rspec = pl.BlockSpec((P_HEADS * P_NKEYS, tb), lambda i, j: (0, i))
    pspec = pl.BlockSpec((P_HEADS, P_NKEYS // 2, tb), lambda i, j: (0, 0, i))
    nb = nb.reshape(P_HEADS * P_NKEYS, n)
    e1 = e1.reshape(P_HEADS * P_NKEYS, n)
    return pl.pallas_call(
        _peer_dense_kernel,
        grid=(n // tb, ne // eb),
        in_specs=[pl.BlockSpec((tb, d), lambda i, j: (i, 0)),
                  pl.BlockSpec((eb, d), lambda i, j: (j, 0)),
                  pl.BlockSpec((d, eb), lambda i, j: (0, j)),
                  rspec, rspec, pspec, pl.BlockSpec((P_HEADS, P_NKEYS, tb), lambda i, j: (0, 0, i)),
                  pl.BlockSpec((tb, d), lambda i, j: (i, 0))],
        out_specs=pl.BlockSpec((tb, d), lambda i, j: (i, 0)),
        out_shape=jax.ShapeDtypeStruct((n, d), F32),
        scratch_shapes=[pltpu.VMEM((d, tb), F32), pltpu.VMEM((eb, tb), F32), pltpu.VMEM((eb, tb), BF16),
                        pltpu.VMEM((P_HEADS, P_NKEYS, tb), BF16)],
        compiler_params=_params("arbitrary", "arbitrary"),
        name="peer_dense",
    )(xn, u, vt, nb, e1, r2, e2, h)


TOKEN_TILE = 512
PEER_EXPERT_TILE = 16 * P_NKEYS
SAMPLE_BATCH_TILE = 8


def _permute_w_in(w_in):
    d = w_in.shape[0]
    seg = lambda a, b: w_in[:, a:b]
    lr = jnp.pad(seg(1792, 1808), ((0, 0), (0, LANE - B_RANK)))
    cols = [seg(2832, 5904), seg(0, 512), seg(1280, 1792), seg(1808, 2320), seg(2320, 2832), seg(768, 1024),
            seg(1024, 1280), seg(512, 640), seg(640, 768), lr, jnp.zeros((d, IN_COLS - COL_LR - LANE), w_in.dtype)]
    return jnp.concatenate(cols, axis=1).astype(BF16)


def kernel(x_prompt, x_sample, mem_prompt, cache_win_k, cache_win_v, state_gla, cache_mem_k, cache_mem_v, norm1_g, w_in, b_gates, a_qnorm_g, a_knorm_g, a_sinks, rel_bias, b_gate_w2, b_gate_b, b_gn_g, c_qnorm_g, c_knorm_g, mem_norm_g, w_mem_kv, w_br_a, w_br_b, w_br_c, w_out, norm2_g, peer_wq, peer_k1, peer_k2, peer_u, peer_v):
    depth = w_in.shape[0]
    assert depth == 1, "one layer"
    l = 0
    bp, tp, d = x_prompt.shape
    bs, ts, _ = x_sample.shape
    cw = C_HEADS * C_HD
    kvw = A_KV_HEADS * A_HD

    w_all = _permute_w_in(w_in[l])
    w2p = jnp.pad(b_gate_w2[l], ((0, LANE - B_RANK), (0, 0))).astype(BF16)
    wa, wb, wc, wo = (w.astype(BF16) for w in (w_br_a[l], w_br_b[l], w_br_c[l], w_out[l]))
    wq_t = peer_wq[l].T.astype(BF16)
    k1, k2 = peer_k1[l].astype(BF16), peer_k2[l].astype(BF16)
    u = peer_u[l].astype(BF16)
    vt = peer_v[l].T.astype(BF16)

    def channel_mix(oa, ob, oc, proj, x2):
        tm = min(TOKEN_TILE, x2.shape[0])
        h, xn2 = merge(oa, ob, oc, proj, b_gates[l], x2, wa, wb, wc, wo, norm2_g[l], tm)
        nb, e1, r2, e2 = peer_route(xn2, wq_t, k1, k2, tm)
        return peer_dense(xn2, u, vt, nb, e1, r2, e2, h, tm, PEER_EXPERT_TILE)

    xp = x_prompt.reshape(bp * tp, d)
    proj_p = norm_matmul(xp, norm1_g[l], w_all, min(TOKEN_TILE, bp * tp), IN_COLS // 2)
    memkv = norm_matmul(mem_prompt.reshape(bp * N_MEM, d), mem_norm_g[l], w_mem_kv[l].astype(BF16), min(TOKEN_TILE, bp * N_MEM), 2 * cw)
    oa_p, wk_p, wv_p = swa_prompt(proj_p, bp, tp, rel_bias, a_qnorm_g[l], a_knorm_g[l], a_sinks[l])
    nchunk = tp // B_CHUNK
    ob_p, st_p = gla(proj_p.reshape(bp, nchunk, B_CHUNK, IN_COLS), jnp.zeros((bp, B_HEADS, B_DV, B_DK), F32),
                     bp, w2p, b_gate_b[l], b_gn_g[l])
    memkv3 = memkv.reshape(bp, N_MEM, 2 * cw)
    oc_p, mk_p = xattn(proj_p.reshape(bp, tp, IN_COLS), memkv3, c_qnorm_g[l], c_knorm_g[l], min(TOKEN_TILE, tp))
    y_p = channel_mix(oa_p, ob_p.reshape(bp * tp, -1), oc_p.reshape(bp * tp, cw), proj_p, xp)

    xs = x_sample.reshape(bs * ts, d)
    proj_s = norm_matmul(xs, norm1_g[l], w_all, min(TOKEN_TILE, bs * ts), IN_COLS // 2)
    proj_s3 = proj_s.reshape(bs, ts, IN_COLS)
    ck = cache_win_k[l].reshape(bs, -1, kvw)
    cv = cache_win_v[l].reshape(bs, -1, kvw)
    oa_s, knew = swa_sample(proj_s3, ck, cv, rel_bias, a_qnorm_g[l], a_knorm_g[l], a_sinks[l], SAMPLE_BATCH_TILE)
    vnew = proj_s3[:, :, COL_AV:COL_AV + kvw]
    ob_s, st_s = gla(proj_s3.reshape(bs, 1, ts, IN_COLS), jnp.swapaxes(state_gla[l], -1, -2),
                     math.gcd(bs, SAMPLE_BATCH_TILE), w2p, b_gate_b[l], b_gn_g[l])
    oc_s = xattn_decode(proj_s3, cache_mem_k[l].reshape(bs, N_MEM * C_HEADS, C_HD),
                        cache_mem_v[l].reshape(bs, N_MEM * C_HEADS, C_HD), c_qnorm_g[l],
                        math.gcd(bs, SAMPLE_BATCH_TILE))
    y_s = channel_mix(oa_s.reshape(bs * ts, -1), ob_s.reshape(bs * ts, -1), oc_s.reshape(bs * ts, cw), proj_s, xs)

    wc_len = ck.shape[1]
    new_wk_s = jnp.concatenate([ck, knew], axis=1)[:, -wc_len:]
    new_wv_s = jnp.concatenate([cv, vnew], axis=1)[:, -wc_len:]
    return (y_p.reshape(bp, tp, d), y_s.reshape(bs, ts, d),
            wk_p.reshape(1, bp, WINDOW, A_KV_HEADS, A_HD), wv_p.reshape(1, bp, WINDOW, A_KV_HEADS, A_HD),
            jnp.swapaxes(st_p, -1, -2)[None],
            mk_p.reshape(1, bp, N_MEM, C_HEADS, C_HD), memkv3[:, :, cw:].reshape(1, bp, N_MEM, C_HEADS, C_HD),
            new_wk_s.reshape(1, bs, wc_len, A_KV_HEADS, A_HD), new_wv_s.reshape(1, bs, wc_len, A_KV_HEADS, A_HD),
            jnp.swapaxes(st_s, -1, -2)[None])
```

```python
import functools
import math

import numpy as np
import jax
import jax.numpy as jnp
from jax import lax
from jax.experimental import pallas as pl
from jax.experimental.pallas import tpu as pltpu

F32 = jnp.float32
BF16 = jnp.bfloat16
I32 = jnp.int32

EPS = 1e-6
D_MODEL = 1024
N_MEM = 256
A_HEADS, A_KV_HEADS, A_GROUP, A_HD = 8, 2, 4, 64
WINDOW = 128
N_BUCKETS, MAX_DISTANCE = 32, 128
B_HEADS, B_DK, B_DV, B_RANK, B_TAU = 4, 64, 128, 16, 16.0
B_CHUNK, B_SUB = 64, 16
C_HEADS, C_HD = 4, 128
P_HEADS, P_NKEYS, P_TOPK = 8, 128, 16
P_EXPERTS = P_NKEYS * P_NKEYS

LANE = 128
COL_GL, COL_AQ, COL_BV, COL_BR, COL_CQ, COL_BQ, COL_BK, COL_AK, COL_AV, COL_LR = (
    0, 3072, 3584, 4096, 4608, 5120, 5376, 5632, 5760, 5888)
IN_COLS = 6144
VMEM_LIMIT = 56 * 1024 * 1024

NT_DIMS = (((1,), (1,)), ((), ()))
TN_DIMS = (((0,), (0,)), ((), ()))


def _rms(x, g):
    ms = jnp.mean(x * x, axis=-1, keepdims=True)
    return x * lax.rsqrt(ms + EPS) * g


def _params(*sem, flags=None):
    return pltpu.CompilerParams(dimension_semantics=sem, vmem_limit_bytes=VMEM_LIMIT, flags=flags)


def _norm_matmul_kernel(x_ref, g_ref, w_ref, o_ref):
    xn = _rms(x_ref[...], g_ref[...]).astype(BF16)
    o_ref[...] = jnp.dot(xn, w_ref[...], preferred_element_type=F32)


def norm_matmul(x, g, w, tm, tn):
    n, d = x.shape
    m = w.shape[1]
    return pl.pallas_call(
        _norm_matmul_kernel,
        grid=(m // tn, n // tm),
        in_specs=[pl.BlockSpec((tm, d), lambda j, i: (i, 0)),
                  pl.BlockSpec((1, d), lambda j, i: (0, 0)),
                  pl.BlockSpec((d, tn), lambda j, i: (0, j))],
        out_specs=pl.BlockSpec((tm, tn), lambda j, i: (i, j)),
        out_shape=jax.ShapeDtypeStruct((n, m), F32),
        compiler_params=_params("arbitrary", "arbitrary"),
        name="norm_matmul",
    )(x, g.reshape(1, d), w)


def _t5_bucket_np(dist):
    n = np.maximum(dist, 0)
    max_exact = N_BUCKETS // 2
    nf = np.maximum(n, 1).astype(np.float32)
    large = max_exact + (np.log(nf / np.float32(max_exact)) / np.float32(math.log(MAX_DISTANCE / max_exact))
                         * np.float32(N_BUCKETS - max_exact)).astype(np.int32)
    large = np.minimum(large, N_BUCKETS - 1)
    return np.where(n < max_exact, n, large).astype(np.int32)


def _bias_from_buckets(bucket, rb_ref, head):
    tab = jnp.zeros(bucket.shape, F32)
    for bk in range(N_BUCKETS):
        tab = jnp.where(bucket == bk, rb_ref[bk, head], tab)
    return tab


def _swa_prompt_kernel(rb_ref, q_ref, kc_ref, kp_ref, vc_ref, vp_ref, gq_ref, gk_ref, snk_ref, bucket_ref,
                       o_ref, wk_ref, wv_ref, bias_ref):
    blk = WINDOW
    first = jnp.logical_and(pl.program_id(0) == 0, pl.program_id(1) == 0)

    @pl.when(first)
    def _():
        bucket = bucket_ref[...]
        qi = lax.broadcasted_iota(I32, (blk, 2 * blk), 0)
        kj = lax.broadcasted_iota(I32, (blk, 2 * blk), 1)
        dist = qi + blk - kj
        band = jnp.logical_and(dist >= 0, dist < WINDOW)
        for h in range(A_HEADS):
            bias_ref[h] = jnp.where(band, _bias_from_buckets(bucket, rb_ref, h), -jnp.inf)

    i = pl.program_id(1)
    nsub = q_ref.shape[0] // blk
    gq, gk = gq_ref[...], gk_ref[...]
    col = lax.broadcasted_iota(I32, (A_GROUP * blk, 2 * blk), 1)
    keep_first = jnp.logical_or(col >= blk, i > 0)
    scale = A_HD ** -0.5
    ones = jnp.ones((2 * blk, A_HD), BF16)
    kn = [[_rms(kp_ref[:, A_HD * g:A_HD * (g + 1)], gk)] +
          [_rms(kc_ref[blk * sb:blk * (sb + 1), A_HD * g:A_HD * (g + 1)], gk) for sb in range(nsub)]
          for g in range(A_KV_HEADS)]
    vv = [[vp_ref[:, A_HD * g:A_HD * (g + 1)]] +
          [vc_ref[blk * sb:blk * (sb + 1), A_HD * g:A_HD * (g + 1)] for sb in range(nsub)]
          for g in range(A_KV_HEADS)]
    for sb in range(nsub):
        q = q_ref[blk * sb:blk * (sb + 1), :]
        outs = [None] * A_HEADS
        for g in range(A_KV_HEADS):
            kcat = jnp.concatenate([kn[g][sb], kn[g][sb + 1]], axis=0).astype(BF16)
            vext = jnp.concatenate([jnp.concatenate([vv[g][sb], vv[g][sb + 1]], axis=0).astype(BF16), ones], axis=1)
            qg = jnp.concatenate(
                [_rms(q[:, A_HD * (A_GROUP * g + j):A_HD * (A_GROUP * g + j + 1)], gq) for j in range(A_GROUP)],
                axis=0)
            s = lax.dot_general(qg.astype(BF16), kcat, NT_DIMS, preferred_element_type=F32) * scale
            s = s + bias_ref[A_GROUP * g:A_GROUP * (g + 1)].reshape(A_GROUP * blk, 2 * blk)
            if sb == 0:
                s = jnp.where(keep_first, s, -jnp.inf)
            snk = snk_ref[g]
            m = jnp.maximum(jnp.max(s, axis=-1, keepdims=True), snk)
            p = jnp.exp(s - m)
            oext = jnp.dot(p.astype(BF16), vext, preferred_element_type=F32)
            den = oext[:, A_HD:A_HD + 1] + jnp.exp(snk - m)
            o = oext[:, :A_HD] / den
            for j in range(A_GROUP):
                outs[A_GROUP * g + j] = o[blk * j:blk * (j + 1)]
        o_ref[blk * sb:blk * (sb + 1), :] = jnp.concatenate(outs, axis=-1)
    wk_ref[0] = jnp.concatenate([kn[g][nsub] for g in range(A_KV_HEADS)], axis=-1)
    wv_ref[0] = vc_ref[blk * (nsub - 1):blk * nsub, :]


SWA_QUERY_BLOCKS = 4


def swa_prompt(proj, batch, seq, rel_bias, gq, gk, sinks):
    blk = WINDOW
    nsub = math.gcd(SWA_QUERY_BLOCKS, seq // blk)
    tq = nsub * blk
    nb = seq // tq
    qi = np.arange(blk)[:, None]
    kj = np.arange(2 * blk)[None, :]
    bucket = jnp.asarray(_t5_bucket_np(qi + blk - kj))
    snk = jnp.broadcast_to(sinks.astype(F32).reshape(A_KV_HEADS, A_GROUP, 1, 1),
                           (A_KV_HEADS, A_GROUP, blk, 1)).reshape(A_KV_HEADS, A_GROUP * blk, 1)
    cur = lambda c: (lambda b, i: (b * nb + i, c))
    prev = lambda c: (lambda b, i: (jnp.maximum((b * nb + i) * nsub - 1, b * nb * nsub), c))
    kvw = A_KV_HEADS * A_HD
    return pl.pallas_call(
        _swa_prompt_kernel,
        grid=(batch, nb),
        in_specs=[pl.BlockSpec(memory_space=pltpu.SMEM),
                  pl.BlockSpec((tq, A_HEADS * A_HD), cur(COL_AQ // (A_HEADS * A_HD))),
                  pl.BlockSpec((tq, kvw), cur(COL_AK // kvw)),
                  pl.BlockSpec((blk, kvw), prev(COL_AK // kvw)),
                  pl.BlockSpec((tq, kvw), cur(COL_AV // kvw)),
                  pl.BlockSpec((blk, kvw), prev(COL_AV // kvw)),
                  pl.BlockSpec((1, A_HD), lambda b, i: (0, 0)),
                  pl.BlockSpec((1, A_HD), lambda b, i: (0, 0)),
                  pl.BlockSpec((A_KV_HEADS, A_GROUP * blk, 1), lambda b, i: (0, 0, 0)),
                  pl.BlockSpec((blk, 2 * blk), lambda b, i: (0, 0))],
        out_specs=[pl.BlockSpec((tq, A_HEADS * A_HD), lambda b, i: (b * nb + i, 0)),
                   pl.BlockSpec((1, blk, kvw), lambda b, i: (b, 0, 0)),
                   pl.BlockSpec((1, blk, kvw), lambda b, i: (b, 0, 0))],
        out_shape=[jax.ShapeDtypeStruct((batch * seq, A_HEADS * A_HD), F32),
                   jax.ShapeDtypeStruct((batch, blk, kvw), F32),
                   jax.ShapeDtypeStruct((batch, blk, kvw), F32)],
        scratch_shapes=[pltpu.VMEM((A_HEADS, blk, 2 * blk), F32)],
        compiler_params=_params("arbitrary", "arbitrary"),
        name="swa_prompt",
    )(rel_bias.astype(F32), proj, proj, proj, proj, proj, gq.reshape(1, A_HD), gk.reshape(1, A_HD), snk, bucket)


def _swa_sample_kernel(rb_ref, q_ref, kn_ref, vn_ref, ck_ref, cv_ref, gq_ref, gk_ref, snk_ref, bkt_ref, valid_ref,
                       o_ref, knew_ref, bias_ref, *, t_new):
    bb = q_ref.shape[0]
    wc = ck_ref.shape[1]
    nrow = A_GROUP * t_new
    pad = bias_ref.shape[2] - wc - t_new

    @pl.when(pl.program_id(0) == 0)
    def _():
        row = lax.broadcasted_iota(I32, bkt_ref.shape, 0)
        for g in range(A_KV_HEADS):
            tab = jnp.zeros(bkt_ref.shape, F32)
            for j in range(A_GROUP):
                in_head = jnp.logical_and(row >= j * t_new, row < (j + 1) * t_new)
                tab = jnp.where(in_head, _bias_from_buckets(bkt_ref[...], rb_ref, A_GROUP * g + j), tab)
            bias_ref[g] = jnp.where(valid_ref[...] > 0.5, tab, -jnp.inf)

    gq, gk = gq_ref[...], gk_ref[...]
    scale = A_HD ** -0.5
    zrows = jnp.zeros((pad, A_HD), F32)
    for i in range(bb):
        q = q_ref[i]
        outs = [None] * A_HEADS
        kn_all = []
        for g in range(A_KV_HEADS):
            ks = slice(A_HD * g, A_HD * (g + 1))
            knn = _rms(kn_ref[i, :, ks], gk)
            kn_all.append(knn)
            kfull = jnp.concatenate([ck_ref[i, :, ks], knn, zrows], axis=0).astype(BF16)
            vfull = jnp.concatenate([cv_ref[i, :, ks], vn_ref[i, :, ks], zrows], axis=0).astype(BF16)
            qg = jnp.concatenate(
                [_rms(q[:, A_HD * (A_GROUP * g + j):A_HD * (A_GROUP * g + j + 1)], gq) for j in range(A_GROUP)],
                axis=0)
            s = lax.dot_general(qg.astype(BF16), kfull, NT_DIMS, preferred_element_type=F32) * scale + bias_ref[g]
            snk = snk_ref[g]
            m = jnp.maximum(jnp.max(s, axis=-1, keepdims=True), snk)
            p = jnp.exp(s - m)
            den = jnp.sum(p, axis=-1, keepdims=True) + jnp.exp(snk - m)
            o = jnp.dot(p.astype(BF16), vfull, preferred_element_type=F32) / den
            for j in range(A_GROUP):
                outs[A_GROUP * g + j] = o[t_new * j:t_new * (j + 1)]
        o_ref[i] = jnp.concatenate(outs, axis=-1)
        knew_ref[i] = jnp.concatenate(kn_all, axis=-1)


def swa_sample(proj3, cache_k, cache_v, rel_bias, gq, gk, sinks, bb):
    batch, t_new, _ = proj3.shape
    wc = cache_k.shape[1]
    kvw = A_KV_HEADS * A_HD
    nrow = A_GROUP * t_new
    ncol = wc + 2 * SUBLANES * (-(-t_new // (2 * SUBLANES)))
    t = (np.arange(nrow) % t_new)[:, None]
    c = np.arange(ncol)[None, :]
    dist = np.where(c < wc, wc + t - c, t - (c - wc))
    valid = (dist >= 0) & (dist < WINDOW) & (c < wc + t_new)
    bkt = jnp.asarray(_t5_bucket_np(dist))
    snk = jnp.broadcast_to(sinks.astype(F32).reshape(A_KV_HEADS, A_GROUP, 1, 1),
                           (A_KV_HEADS, A_GROUP, t_new, 1)).reshape(A_KV_HEADS, nrow, 1)
    whole = lambda *shape: pl.BlockSpec(shape, lambda i: (0,) * len(shape))
    return pl.pallas_call(
        functools.partial(_swa_sample_kernel, t_new=t_new),
        grid=(batch // bb,),
        in_specs=[pl.BlockSpec(memory_space=pltpu.SMEM),
                  pl.BlockSpec((bb, t_new, A_HEADS * A_HD), lambda i: (i, 0, COL_AQ // (A_HEADS * A_HD))),
                  pl.BlockSpec((bb, t_new, kvw), lambda i: (i, 0, COL_AK // kvw)),
                  pl.BlockSpec((bb, t_new, kvw), lambda i: (i, 0, COL_AV // kvw)),
                  pl.BlockSpec((bb, wc, kvw), lambda i: (i, 0, 0)),
                  pl.BlockSpec((bb, wc, kvw), lambda i: (i, 0, 0)),
                  whole(1, A_HD), whole(1, A_HD), whole(A_KV_HEADS, nrow, 1), whole(nrow, ncol), whole(nrow, ncol)],
        out_specs=[pl.BlockSpec((bb, t_new, A_HEADS * A_HD), lambda i: (i, 0, 0)),
                   pl.BlockSpec((bb, t_new, kvw), lambda i: (i, 0, 0))],
        out_shape=[jax.ShapeDtypeStruct((batch, t_new, A_HEADS * A_HD), F32),
                   jax.ShapeDtypeStruct((batch, t_new, kvw), F32)],
        scratch_shapes=[pltpu.VMEM((A_KV_HEADS, nrow, ncol), F32)],
        compiler_params=_params("arbitrary"),
        name="swa_sample",
    )(rel_bias.astype(F32), proj3, proj3, proj3, cache_k, cache_v,
      gq.reshape(1, A_HD), gk.reshape(1, A_HD), snk, bkt, jnp.asarray(valid.astype(np.float32)))


def _gla_kernel(q_ref, k_ref, v_ref, lr_ref, r_ref, s0_ref, w2_ref, gb_ref, gn_ref, o_ref, sout_ref, st_ref,
                *, chunk, sub):
    c = pl.program_id(1)
    bb = q_ref.shape[0]

    @pl.when(c == 0)
    def _():
        st_ref[...] = s0_ref[...]

    gn = gn_ref[...]
    row = lax.broadcasted_iota(I32, (chunk, chunk), 0)
    col = lax.broadcasted_iota(I32, (chunk, chunk), 1)
    nsub = chunk // sub
    for i in range(bb):
        z = jnp.dot(lr_ref[i, 0].astype(BF16), w2_ref[...], preferred_element_type=F32) + gb_ref[...]
        b = (jnp.minimum(z, 0.0) - jnp.log1p(jnp.exp(-jnp.abs(z)))) / B_TAU
        row_w = lax.broadcasted_iota(I32, b.shape, 0)
        shift = 1
        while shift < chunk:
            b = b + jnp.where(row_w >= shift, pltpu.roll(b, shift, 0), 0.0)
            shift *= 2
        q = q_ref[i, 0] * (B_DK ** -0.5)
        k, v, r = k_ref[i, 0], v_ref[i, 0], r_ref[i, 0]
        outs = []
        for h in range(B_HEADS):
            ks = slice(B_DK * h, B_DK * (h + 1))
            vs = slice(B_DV * h, B_DV * (h + 1))
            qh, kh, vh, bh = q[:, ks], k[:, ks], v[:, vs], b[:, ks]
            st = st_ref[i, h]
            o = lax.dot_general((qh * jnp.exp(bh)).astype(BF16), st.astype(BF16), NT_DIMS,
                                preferred_element_type=F32)
            refs = [bh[sub * (j + 1) - 1:sub * (j + 1)] for j in range(nsub)]
            rfull = jnp.concatenate([jnp.broadcast_to(rj, (sub, B_DK)) for rj in refs], axis=0)
            kt = (kh * jnp.exp(rfull - bh)).astype(BF16)
            qcat = jnp.concatenate([qh * jnp.exp(jnp.minimum(bh - rj, 80.0)) for rj in refs], axis=0).astype(BF16)
            aa = lax.dot_general(qcat, kt, NT_DIMS, preferred_element_type=F32)
            att = jnp.zeros((chunk, chunk), F32)
            for j in range(nsub):
                in_sub = jnp.logical_and(col >= sub * j, col < sub * (j + 1))
                att = jnp.where(in_sub, aa[chunk * j:chunk * (j + 1)], att)
            att = jnp.where(col <= row, att, 0.0)
            o = o + jnp.dot(att.astype(BF16), vh.astype(BF16), preferred_element_type=F32)
            bl = bh[chunk - 1:chunk]
            kd = (kh * jnp.exp(bl - bh)).astype(BF16)
            st_ref[i, h] = st * jnp.exp(bl) + lax.dot_general(vh.astype(BF16), kd, TN_DIMS,
                                                              preferred_element_type=F32)
            rh = r[:, vs]
            outs.append(_rms(o, gn) * (rh * jax.nn.sigmoid(rh)))
        o_ref[i, 0] = jnp.concatenate(outs, axis=-1)
    sout_ref[...] = st_ref[...]


def gla(proj4, state_t, bb, w2p, gbias, gn):
    batch, nchunk, chunk, _ = proj4.shape
    sub = min(B_SUB, chunk)
    kw, vw = B_HEADS * B_DK, B_HEADS * B_DV
    blk = lambda w, col: pl.BlockSpec((bb, 1, chunk, w), lambda b, c: (b, c, 0, col // w))
    whole = lambda *shape: pl.BlockSpec(shape, lambda b, c: (0,) * len(shape))
    st_spec = pl.BlockSpec((bb, B_HEADS, B_DV, B_DK), lambda b, c: (b, 0, 0, 0))
    return pl.pallas_call(
        functools.partial(_gla_kernel, chunk=chunk, sub=sub),
        grid=(batch // bb, nchunk),
        in_specs=[blk(kw, COL_BQ), blk(kw, COL_BK), blk(vw, COL_BV), blk(LANE, COL_LR), blk(vw, COL_BR), st_spec,
                  whole(LANE, kw), whole(1, kw), whole(1, B_DV)],
        out_specs=[pl.BlockSpec((bb, 1, chunk, vw), lambda b, c: (b, c, 0, 0)), st_spec],
        out_shape=[jax.ShapeDtypeStruct((batch, nchunk, chunk, vw), F32),
                   jax.ShapeDtypeStruct((batch, B_HEADS, B_DV, B_DK), F32)],
        scratch_shapes=[pltpu.VMEM((bb, B_HEADS, B_DV, B_DK), F32)],
        compiler_params=_params("arbitrary", "arbitrary"),
        name="gla",
    )(proj4, proj4, proj4, proj4, proj4, state_t, w2p, gbias.reshape(1, kw), gn.reshape(1, B_DV))


def _xattn_kernel(q_ref, k_ref, v_ref, gq_ref, gk_ref, o_ref, kn_ref):
    gq, gk = gq_ref[...], gk_ref[...]
    scale = C_HD ** -0.5
    q = q_ref[0]
    outs, kns = [], []
    for h in range(C_HEADS):
        hs = slice(C_HD * h, C_HD * (h + 1))
        kh = _rms(k_ref[0, :, hs], gk)
        kns.append(kh)
        qh = _rms(q[:, hs], gq)
        s = lax.dot_general(qh.astype(BF16), kh.astype(BF16), NT_DIMS, preferred_element_type=F32) * scale
        p = jnp.exp(s - jnp.max(s, axis=-1, keepdims=True))
        p = p / jnp.sum(p, axis=-1, keepdims=True)
        outs.append(jnp.dot(p.astype(BF16), v_ref[0, :, hs].astype(BF16), preferred_element_type=F32))
    o_ref[0] = jnp.concatenate(outs, axis=-1)
    kn_ref[0] = jnp.concatenate(kns, axis=-1)


def xattn(proj3, mem_kv, gq, gk, tq):
    batch, t, _ = proj3.shape
    cw = C_HEADS * C_HD
    whole = lambda *shape: pl.BlockSpec(shape, lambda b, i: (0,) * len(shape))
    return pl.pallas_call(
        _xattn_kernel,
        grid=(batch, t // tq),
        in_specs=[pl.BlockSpec((1, tq, cw), lambda b, i: (b, i, COL_CQ // cw)),
                  pl.BlockSpec((1, N_MEM, cw), lambda b, i: (b, 0, 0)),
                  pl.BlockSpec((1, N_MEM, cw), lambda b, i: (b, 0, 1)),
                  whole(1, C_HD), whole(1, C_HD)],
        out_specs=[pl.BlockSpec((1, tq, cw), lambda b, i: (b, i, 0)),
                   pl.BlockSpec((1, N_MEM, cw), lambda b, i: (b, 0, 0))],
        out_shape=[jax.ShapeDtypeStruct((batch, t, cw), F32), jax.ShapeDtypeStruct((batch, N_MEM, cw), F32)],
        compiler_params=_params("arbitrary", "arbitrary"),
        name="xattn",
    )(proj3, mem_kv, mem_kv, gq.reshape(1, C_HD), gk.reshape(1, C_HD))


def _xattn_decode_kernel(q_ref, k_ref, v_ref, gq_ref, o_ref, *, bb):
    tq = q_ref.shape[1]
    nrow, ncol = C_HEADS * tq, k_ref.shape[1]
    row = lax.broadcasted_iota(I32, (nrow, ncol), 0)
    col = lax.broadcasted_iota(I32, (nrow, ncol), 1)
    row_head = jnp.zeros((nrow, ncol), I32)
    for h in range(1, C_HEADS):
        row_head = row_head + jnp.where(row >= h * tq, 1, 0)
    same_head = (col & (C_HEADS - 1)) == row_head
    gq = gq_ref[...]
    scale = C_HD ** -0.5
    for i in range(bb):
        q = q_ref[i]
        qb = jnp.concatenate([_rms(q[:, C_HD * h:C_HD * (h + 1)], gq) for h in range(C_HEADS)], axis=0)
        s = lax.dot_general(qb.astype(BF16), k_ref[i].astype(BF16), NT_DIMS, preferred_element_type=F32) * scale
        s = jnp.where(same_head, s, -jnp.inf)
        p = jnp.exp(s - jnp.max(s, axis=-1, keepdims=True))
        o = jnp.dot(p.astype(BF16), v_ref[i].astype(BF16), preferred_element_type=F32)
        o = o / jnp.sum(p, axis=-1, keepdims=True)
        o_ref[i] = jnp.concatenate([o[tq * h:tq * (h + 1)] for h in range(C_HEADS)], axis=-1)


def xattn_decode(proj3, mem_k, mem_v, gq, bb):
    assert C_HEADS & (C_HEADS - 1) == 0
    batch, tq, _ = proj3.shape
    cw = C_HEADS * C_HD
    kspec = pl.BlockSpec((bb, mem_k.shape[1], C_HD), lambda b: (b, 0, 0))
    return pl.pallas_call(
        functools.partial(_xattn_decode_kernel, bb=bb),
        grid=(batch // bb,),
        in_specs=[pl.BlockSpec((bb, tq, cw), lambda b: (b, 0, COL_CQ // cw)), kspec, kspec,
                  pl.BlockSpec((1, C_HD), lambda b: (0, 0))],
        out_specs=pl.BlockSpec((bb, tq, cw), lambda b: (b, 0, 0)),
        out_shape=jax.ShapeDtypeStruct((batch, tq, cw), F32),
        compiler_params=_params("arbitrary"),
        name="xattn_decode",
    )(proj3, mem_k, mem_v, gq.reshape(1, C_HD))


def _merge_kernel(oa_ref, ob_ref, oc_ref, gl_ref, bg_ref, x_ref, wa_ref, wb_ref, wc_ref, wo_ref, g2_ref,
                  h_ref, xn_ref):
    d = D_MODEL
    gate = jax.nn.sigmoid(gl_ref[...] + bg_ref[...])
    pa = jnp.dot(oa_ref[...].astype(BF16), wa_ref[...], preferred_element_type=F32)
    pb = jnp.dot(ob_ref[...].astype(BF16), wb_ref[...], preferred_element_type=F32)
    pc = jnp.dot(oc_ref[...].astype(BF16), wc_ref[...], preferred_element_type=F32)
    merged = gate[:, :d] * pa + gate[:, d:2 * d] * pb + gate[:, 2 * d:] * pc
    h = x_ref[...] + jnp.dot(merged.astype(BF16), wo_ref[...], preferred_element_type=F32)
    h_ref[...] = h
    xn_ref[...] = _rms(h, g2_ref[...]).astype(BF16)


def merge(oa, ob, oc, proj, b_gates, x, wa, wb, wc, wo, g2, tm):
    n, d = x.shape
    bw = oa.shape[1]
    tok = lambda w: pl.BlockSpec((tm, w), lambda i: (i, 0))
    whole = lambda *shape: pl.BlockSpec(shape, lambda i: (0,) * len(shape))
    return pl.pallas_call(
        _merge_kernel,
        grid=(n // tm,),
        in_specs=[tok(bw), tok(bw), tok(bw), tok(3 * d), whole(1, 3 * d), tok(d),
                  whole(bw, d), whole(bw, d), whole(bw, d), whole(d, d), whole(1, d)],
        out_specs=[tok(d), tok(d)],
        out_shape=[jax.ShapeDtypeStruct((n, d), F32), jax.ShapeDtypeStruct((n, d), BF16)],
        compiler_params=_params("arbitrary"),
        name="merge",
    )(oa, ob, oc, proj, b_gates.reshape(1, 3 * d), x, wa, wb, wc, wo, g2.reshape(1, d))


_UNRANKED = float(P_NKEYS)


def _top16_exact(s):
    nrow = s.shape[0]
    row = lax.broadcasted_iota(I32, s.shape, 0)
    rank = jnp.full(s.shape, _UNRANKED, F32)
    vals = []
    for kk in range(P_TOPK):
        m = jnp.max(s, axis=0, keepdims=True)
        idx = jnp.min(jnp.where(s == m, row, nrow), axis=0, keepdims=True)
        hit = row == idx
        rank = jnp.where(hit, float(kk), rank)
        s = jnp.where(hit, -jnp.inf, s)
        vals.append(m)
    return jnp.concatenate(vals, axis=0), rank


def _top16_distinct(s, want_rank=True):
    rank = jnp.zeros(s.shape, F32)
    m = jnp.max(s, axis=0, keepdims=True)
    vals = [m]
    for kk in range(1, P_TOPK):
        below = s < m
        if want_rank:
            rank = jnp.where(below, float(kk), rank)
        m = jnp.max(jnp.where(below, s, -jnp.inf), axis=0, keepdims=True)
        vals.append(m)
    below = s < m
    rank = jnp.where(below, _UNRANKED, rank)
    count = jnp.sum(jnp.where(below, 0.0, 1.0), axis=0, keepdims=True)
    return jnp.concatenate(vals, axis=0), rank, below, count == float(P_TOPK)


_CAND_ROWS = 16 + 7 * 8 + 8


def _candidates(v1, v2):
    blocks = [v1[0:1] + v2[0:16]]
    blocks += [v1[a:a + 1] + v2[0:8] for a in range(1, 8)]
    blocks.append(v1[8:16] + v2[0:1])
    return jnp.concatenate(blocks, axis=0)


SUBLANES = 8


def _compare_exchange(v, i, l):
    v[i], v[l] = jnp.maximum(v[i], v[l]), jnp.minimum(v[i], v[l])


def _sorted_top16_slabs(s):
    nslab = s.shape[0] // SUBLANES
    assert nslab == P_TOPK
    v = [s[SUBLANES * i:SUBLANES * (i + 1)] for i in range(nslab)]
    k = 2
    while k <= nslab:
        j = k // 2
        while j >= 1:
            for i in range(nslab):
                l = i ^ j
                if l > i:
                    if (i & k) == 0:
                        _compare_exchange(v, i, l)
                    else:
                        _compare_exchange(v, l, i)
            j //= 2
        k *= 2
    shift = SUBLANES // 2
    while shift >= 1:
        w = [pltpu.roll(x, shift, 0) for x in v]
        v = [jnp.maximum(v[i], w[nslab - 1 - i]) for i in range(nslab)]
        j = nslab // 2
        while j >= 1:
            for i in range(nslab):
                if (i ^ j) > i:
                    _compare_exchange(v, i, i ^ j)
            j //= 2
        shift //= 2
    return v


def _slabs(x):
    return [x[SUBLANES * i:SUBLANES * (i + 1)] for i in range(x.shape[0] // SUBLANES)]


def _pair_counts(cand, sel, cmax):
    zsum = jnp.sum(jnp.where(sel, jnp.exp(cand - cmax), 0.0), axis=0, keepdims=True)
    self32 = jnp.where(sel, 1.0, 0.0)
    counts = [jnp.sum(self32[0:16], axis=0, keepdims=True)]
    counts += [jnp.sum(self32[16 + 8 * (a - 1):16 + 8 * a], axis=0, keepdims=True) for a in range(1, 8)]
    counts += [self32[72 + a - 8:73 + a - 8] for a in range(8, 16)]
    return zsum, counts


def _route_store(refs, ls, s1, s2, top1, top2, nb, in1, rank2, zsum):
    nb_ref, e1_ref, r2_ref, e2_ref = refs
    nb_ref[:, ls] = nb
    e1_ref[:, ls] = jnp.where(in1, jnp.exp(s1 - top1), 0.0)
    e2_ref[:, ls] = jnp.where(rank2 < float(P_TOPK), jnp.exp(s2 - top2) * (0.5 / zsum), 0.0)
    r2_ref[:, ls] = pltpu.bitcast(rank2.astype(BF16), jnp.uint32)


def _route_strip_distinct(refs, ls, s1, s2):
    t1 = _sorted_top16_slabs(s1)
    t2 = _sorted_top16_slabs(s2)
    v1 = jnp.concatenate([t[0:1] for t in t1], axis=0)
    v2 = jnp.concatenate([t[0:1] for t in t2], axis=0)
    cand = _candidates(v1, v2)
    _, _, cbelow, okc = _top16_distinct(cand, want_rank=False)
    zsum, counts = _pair_counts(cand, jnp.logical_not(cbelow), v1[0:1] + v2[0:1])
    nb_rows, rank_rows, n1, n2 = [], [], None, None
    for x1, x2 in zip(_slabs(s1), _slabs(s2)):
        nb = jnp.broadcast_to(counts[0], x1.shape)
        rank = jnp.zeros(x2.shape, F32)
        for a in range(1, P_TOPK):
            nb = jnp.where(x1 < t1[a - 1], counts[a], nb)
            rank = jnp.where(x2 < t2[a - 1], float(a), rank)
        out1, out2 = x1 < t1[P_TOPK - 1], x2 < t2[P_TOPK - 1]
        nb_rows.append(jnp.where(out1, 0.0, nb))
        rank_rows.append(jnp.where(out2, _UNRANKED, rank))
        c1, c2 = jnp.where(out1, 0.0, 1.0), jnp.where(out2, 0.0, 1.0)
        n1, n2 = (c1, c2) if n1 is None else (n1 + c1, n2 + c2)
    nb = jnp.concatenate(nb_rows, axis=0)
    rank2 = jnp.concatenate(rank_rows, axis=0)
    _route_store(refs, ls, s1, s2, v1[0:1], v2[0:1], nb, s1 >= v1[P_TOPK - 1:P_TOPK], rank2, zsum)
    strict = jnp.ones(t1[0].shape, F32)
    for a in range(1, P_TOPK):
        strict = jnp.where(jnp.logical_and(t1[a - 1] > t1[a], t2[a - 1] > t2[a]), strict, 0.0)
    ok1 = jnp.sum(n1, axis=0, keepdims=True) == float(P_TOPK)
    ok2 = jnp.sum(n2, axis=0, keepdims=True) == float(P_TOPK)
    ok = jnp.logical_and(jnp.logical_and(ok1, ok2), okc)
    return jnp.logical_and(ok, strict[0:1] > 0.5)


def _route_strip_exact(refs, ls, s1, s2):
    v1, rank1 = _top16_exact(s1)
    v2, rank2 = _top16_exact(s2)
    cand = _candidates(v1, v2)
    _, crank = _top16_exact(cand)
    zsum, counts = _pair_counts(cand, crank < float(P_TOPK), v1[0:1] + v2[0:1])
    nb = jnp.zeros(s1.shape, F32)
    for a in range(P_TOPK):
        nb = jnp.where(rank1 == float(a), counts[a], nb)
    _route_store(refs, ls, s1, s2, v1[0:1], v2[0:1], nb, rank1 < float(P_TOPK), rank2, zsum)


def _peer_route_kernel(xn_ref, wq_ref, k1_ref, k2_ref, nb_ref, e1_ref, r2_ref, e2_ref, s_ref):
    tb = xn_ref.shape[0]
    nhead = nb_ref.shape[0]
    hw = wq_ref.shape[0] // nhead
    half = hw // 2
    work = []
    for hh in range(nhead):
        qt = lax.dot_general(wq_ref[hw * hh:hw * (hh + 1), :], xn_ref[...], NT_DIMS,
                             preferred_element_type=F32)
        s_ref[2 * hh] = jnp.dot(k1_ref[...], qt[:half].astype(BF16), preferred_element_type=F32)
        s_ref[2 * hh + 1] = jnp.dot(k2_ref[...], qt[half:].astype(BF16), preferred_element_type=F32)
        refs = tuple(r.at[hh] for r in (nb_ref, e1_ref, r2_ref, e2_ref))
        work += [(refs, 2 * hh, slice(LANE * st, LANE * (st + 1))) for st in range(tb // LANE)]
    bad = []
    for refs, si, ls in work:
        ok = _route_strip_distinct(refs, ls, s_ref[si, :, ls], s_ref[si + 1, :, ls])
        bad.append(jnp.max(jnp.where(ok, 0.0, 1.0)) > 0.5)

    for (refs, si, ls), bad_strip in zip(work, bad):
        @pl.when(bad_strip)
        def _(refs=refs, si=si, ls=ls):
            _route_strip_exact(refs, ls, s_ref[si, :, ls], s_ref[si + 1, :, ls])


ROUTE_HEADS_PER_STEP = 2


def peer_route(xn, wq_t, k1, k2, tb):
    n, d = xn.shape
    hw = wq_t.shape[0] // P_HEADS
    hps = ROUTE_HEADS_PER_STEP
    ospec = pl.BlockSpec((hps, P_NKEYS, tb), lambda i, h: (h, 0, i))
    pspec = pl.BlockSpec((hps, P_NKEYS // 2, tb), lambda i, h: (h, 0, i))
    shape = lambda dt: jax.ShapeDtypeStruct((P_HEADS, P_NKEYS, n), dt)
    packed = jax.ShapeDtypeStruct((P_HEADS, P_NKEYS // 2, n), jnp.uint32)
    return pl.pallas_call(
        _peer_route_kernel,
        grid=(n // tb, P_HEADS // hps),
        in_specs=[pl.BlockSpec((tb, d), lambda i, h: (i, 0)),
                  pl.BlockSpec((hps * hw, d), lambda i, h: (h, 0)),
                  pl.BlockSpec((P_NKEYS, hw // 2), lambda i, h: (0, 0)),
                  pl.BlockSpec((P_NKEYS, hw // 2), lambda i, h: (0, 0))],
        out_specs=[ospec, ospec, pspec, ospec],
        out_shape=[shape(F32), shape(F32), packed, shape(F32)],
        scratch_shapes=[pltpu.VMEM((2 * hps, P_NKEYS, tb), F32)],
        compiler_params=_params("arbitrary", "arbitrary"),
        name="peer_route",
    )(xn, wq_t, k1, k2)


MXU_DIM = 256


BF16_ROWS = 16
C_GROUP = 1


def _peer_dense_kernel(xn_ref, u_ref, vt_ref, nb_ref, e1_ref, r2_ref, e2_ref, h_ref, y_ref,
                       acc_ref, hid_ref, act_ref, e2b_ref):
    j = pl.program_id(1)
    tb = xn_ref.shape[0]
    eb = u_ref.shape[0]
    n1 = eb // P_NKEYS
    nrg = P_NKEYS // BF16_ROWS
    half = BF16_ROWS // 2

    @pl.when(j == 0)
    def _():
        acc_ref[...] = jnp.zeros_like(acc_ref)
        for h in range(P_HEADS):
            e2b_ref[h] = e2_ref[h].astype(BF16)

    hid_ref[...] = lax.dot_general(u_ref[...], xn_ref[...], NT_DIMS, preferred_element_type=F32)
    zero = jnp.zeros((), BF16)

    def row_vreg(ref, c, h, ls):
        row = ref[pl.ds(h * P_NKEYS + j * n1 + c, 1), :]
        return jnp.broadcast_to(row[:, ls], (BF16_ROWS, LANE)).astype(BF16)

    for st in range(tb // LANE):
        ls = slice(LANE * st, LANE * (st + 1))
        for c0 in range(0, n1, C_GROUP):
            nbv = [[row_vreg(nb_ref, c0 + k, h, ls) for h in range(P_HEADS)] for k in range(C_GROUP)]
            e1v = [[row_vreg(e1_ref, c0 + k, h, ls) for h in range(P_HEADS)] for k in range(C_GROUP)]
            for rg in range(nrg):
                ws = slice(half * rg, half * (rg + 1))
                gs = [jnp.zeros((BF16_ROWS, LANE), BF16) for _ in range(C_GROUP)]
                for h in range(P_HEADS):
                    r2 = pltpu.bitcast(r2_ref[h, ws, ls], BF16)
                    e2 = e2b_ref[h, BF16_ROWS * rg:BF16_ROWS * (rg + 1), ls]
                    for k in range(C_GROUP):
                        gs[k] = gs[k] + jnp.where(r2 < nbv[k][h], e2, zero) * e1v[k][h]
                for k in range(C_GROUP):
                    rs = slice(P_NKEYS * (c0 + k) + BF16_ROWS * rg, P_NKEYS * (c0 + k) + BF16_ROWS * (rg + 1))
                    hh = hid_ref[rs, ls]
                    gelu2 = hh * (1.0 + lax.erf(hh * math.sqrt(0.5)))
                    act_ref[rs, ls] = gelu2.astype(BF16) * gs[k]
    acc_ref[...] += jnp.dot(vt_ref[...], act_ref[...], preferred_element_type=F32)

    @pl.when(j == pl.num_programs(1) - 1)
    def _():
        y_ref[...] = h_ref[...] + acc_ref[...].T


def peer_dense(xn, u, vt, nb, e1, r2, e2, h, tb, eb):
    n, d = xn.shape
    ne = u.shape[0]
    rspec = pl.BlockSpec((P_HEADS * P_NKEYS, tb), lambda i, j: (0, i))
    pspec = pl.BlockSpec((P_HEADS, P_NKEYS // 2, tb), lambda i, j: (0, 0, i))
    nb = nb.reshape(P_HEADS * P_NKEYS, n)
    e1 = e1.reshape(P_HEADS * P_NKEYS, n)
    return pl.pallas_call(
        _peer_dense_kernel,
        grid=(n // tb, ne // eb),
        in_specs=[pl.BlockSpec((tb, d), lambda i, j: (i, 0)),
                  pl.BlockSpec((eb, d), lambda i, j: (j, 0)),
                  pl.BlockSpec((d, eb), lambda i, j: (0, j)),
                  rspec, rspec, pspec, pl.BlockSpec((P_HEADS, P_NKEYS, tb), lambda i, j: (0, 0, i)),
                  pl.BlockSpec((tb, d), lambda i, j: (i, 0))],
        out_specs=pl.BlockSpec((tb, d), lambda i, j: (i, 0)),
        out_shape=jax.ShapeDtypeStruct((n, d), F32),
        scratch_shapes=[pltpu.VMEM((d, tb), F32), pltpu.VMEM((eb, tb), F32), pltpu.VMEM((eb, tb), BF16),
                        pltpu.VMEM((P_HEADS, P_NKEYS, tb), BF16)],
        compiler_params=_params("arbitrary", "arbitrary"),
        name="peer_dense",
    )(xn, u, vt, nb, e1, r2, e2, h)


TOKEN_TILE = 512
PEER_EXPERT_TILE = 16 * P_NKEYS
SAMPLE_BATCH_TILE = 8


def _permute_w_in(w_in):
    d = w_in.shape[0]
    seg = lambda a, b: w_in[:, a:b]
    lr = jnp.pad(seg(1792, 1808), ((0, 0), (0, LANE - B_RANK)))
    cols = [seg(2832, 5904), seg(0, 512), seg(1280, 1792), seg(1808, 2320), seg(2320, 2832), seg(768, 1024),
            seg(1024, 1280), seg(512, 640), seg(640, 768), lr, jnp.zeros((d, IN_COLS - COL_LR - LANE), w_in.dtype)]
    return jnp.concatenate(cols, axis=1).astype(BF16)


def kernel(x_prompt, x_sample, mem_prompt, cache_win_k, cache_win_v, state_gla, cache_mem_k, cache_mem_v, norm1_g, w_in, b_gates, a_qnorm_g, a_knorm_g, a_sinks, rel_bias, b_gate_w2, b_gate_b, b_gn_g, c_qnorm_g, c_knorm_g, mem_norm_g, w_mem_kv, w_br_a, w_br_b, w_br_c, w_out, norm2_g, peer_wq, peer_k1, peer_k2, peer_u, peer_v):
    depth = w_in.shape[0]
    assert depth == 1, "one layer"
    l = 0
    bp, tp, d = x_prompt.shape
    bs, ts, _ = x_sample.shape
    cw = C_HEADS * C_HD
    kvw = A_KV_HEADS * A_HD

    w_all = _permute_w_in(w_in[l])
    w2p = jnp.pad(b_gate_w2[l], ((0, LANE - B_RANK), (0, 0))).astype(BF16)
    wa, wb, wc, wo = (w.astype(BF16) for w in (w_br_a[l], w_br_b[l], w_br_c[l], w_out[l]))
    wq_t = peer_wq[l].T.astype(BF16)
    k1, k2 = peer_k1[l].astype(BF16), peer_k2[l].astype(BF16)
    u = peer_u[l].astype(BF16)
    vt = peer_v[l].T.astype(BF16)

    def channel_mix(oa, ob, oc, proj, x2):
        tm = min(TOKEN_TILE, x2.shape[0])
        h, xn2 = merge(oa, ob, oc, proj, b_gates[l], x2, wa, wb, wc, wo, norm2_g[l], tm)
        nb, e1, r2, e2 = peer_route(xn2, wq_t, k1, k2, tm)
        return peer_dense(xn2, u, vt, nb, e1, r2, e2, h, tm, PEER_EXPERT_TILE)

    xp = x_prompt.reshape(bp * tp, d)
    proj_p = norm_matmul(xp, norm1_g[l], w_all, min(TOKEN_TILE, bp * tp), IN_COLS // 2)
    memkv = norm_matmul(mem_prompt.reshape(bp * N_MEM, d), mem_norm_g[l], w_mem_kv[l].astype(BF16), min(TOKEN_TILE, bp * N_MEM), 2 * cw)
    oa_p, wk_p, wv_p = swa_prompt(proj_p, bp, tp, rel_bias, a_qnorm_g[l], a_knorm_g[l], a_sinks[l])
    nchunk = tp // B_CHUNK
    ob_p, st_p = gla(proj_p.reshape(bp, nchunk, B_CHUNK, IN_COLS), jnp.zeros((bp, B_HEADS, B_DV, B_DK), F32),
                     bp, w2p, b_gate_b[l], b_gn_g[l])
    memkv3 = memkv.reshape(bp, N_MEM, 2 * cw)
    oc_p, mk_p = xattn(proj_p.reshape(bp, tp, IN_COLS), memkv3, c_qnorm_g[l], c_knorm_g[l], min(TOKEN_TILE, tp))
    y_p = channel_mix(oa_p, ob_p.reshape(bp * tp, -1), oc_p.reshape(bp * tp, cw), proj_p, xp)

    xs = x_sample.reshape(bs * ts, d)
    proj_s = norm_matmul(xs, norm1_g[l], w_all, min(TOKEN_TILE, bs * ts), IN_COLS // 2)
    proj_s3 = proj_s.reshape(bs, ts, IN_COLS)
    ck = cache_win_k[l].reshape(bs, -1, kvw)
    cv = cache_win_v[l].reshape(bs, -1, kvw)
    oa_s, knew = swa_sample(proj_s3, ck, cv, rel_bias, a_qnorm_g[l], a_knorm_g[l], a_sinks[l], SAMPLE_BATCH_TILE)
    vnew = proj_s3[:, :, COL_AV:COL_AV + kvw]
    ob_s, st_s = gla(proj_s3.reshape(bs, 1, ts, IN_COLS), jnp.swapaxes(state_gla[l], -1, -2),
                     math.gcd(bs, SAMPLE_BATCH_TILE), w2p, b_gate_b[l], b_gn_g[l])
    oc_s = xattn_decode(proj_s3, cache_mem_k[l].reshape(bs, N_MEM * C_HEADS, C_HD),
                        cache_mem_v[l].reshape(bs, N_MEM * C_HEADS, C_HD), c_qnorm_g[l],
                        math.gcd(bs, SAMPLE_BATCH_TILE))
    y_s = channel_mix(oa_s.reshape(bs * ts, -1), ob_s.reshape(bs * ts, -1), oc_s.reshape(bs * ts, cw), proj_s, xs)

    wc_len = ck.shape[1]
    new_wk_s = jnp.concatenate([ck, knew], axis=1)[:, -wc_len:]
    new_wv_s = jnp.concatenate([cv, vnew], axis=1)[:, -wc_len:]
    return (y_p.reshape(bp, tp, d), y_s.reshape(bs, ts, d),
            wk_p.reshape(1, bp, WINDOW, A_KV_HEADS, A_HD), wv_p.reshape(1, bp, WINDOW, A_KV_HEADS, A_HD),
            jnp.swapaxes(st_p, -1, -2)[None],
            mk_p.reshape(1, bp, N_MEM, C_HEADS, C_HD), memkv3[:, :, cw:].reshape(1, bp, N_MEM, C_HEADS, C_HD),
            new_wk_s.reshape(1, bs, wc_len, A_KV_HEADS, A_HD), new_wv_s.reshape(1, bs, wc_len, A_KV_HEADS, A_HD),
            jnp.swapaxes(st_s, -1, -2)[None])
```
